```python
import jax, jax.numpy as jnp
from jax import lax
import numpy as np

D_MODEL = 1024
BATCH = 8
SEQ = 4096
DEPTH = 4

CHUNK = 64
N_MIXERS = 2
EPS = 1e-6

HG_DK = 128
HG_HEADS = max(4, D_MODEL // HG_DK)
HG_DV = D_MODEL // HG_HEADS
HG_FDIM = HG_HEADS * HG_DK
HG_WIDTH = HG_HEADS * HG_DV
HG_IN = 2 * HG_FDIM + 2 * HG_WIDTH

FX_HD = 64
FX_HEADS = D_MODEL // FX_HD
FX_WIDTH = FX_HEADS * FX_HD
FX_IN = 4 * FX_WIDTH + FX_HEADS
Q_BLOCK = 128

N_HG_LAYERS = (DEPTH + N_MIXERS - 1) // N_MIXERS
N_FX_LAYERS = DEPTH // N_MIXERS

kernel_name = "hybrid_hgrn2_fox_adaln_trunk"


def _rms(x, g):
    xf = x.astype(jnp.float32)
    y = xf * lax.rsqrt(jnp.mean(xf * xf, axis=-1, keepdims=True) + EPS)
    return (y * g.astype(jnp.float32)).astype(x.dtype)


def _hgrn2_scan(q, log_f, k, v):
    B, S, H, DK = q.shape
    DV = v.shape[-1]
    N = S // CHUNK

    def to_chunks(t):
        return t.astype(jnp.float32).reshape(B, N, CHUNK, H, t.shape[-1]).transpose(1, 0, 3, 2, 4)

    qc, gc, kc, vc = to_chunks(q), to_chunks(log_f), to_chunks(k), to_chunks(v)
    causal = jnp.tril(jnp.ones((CHUNK, CHUNK), dtype=bool))

    def step(state, inp):
        qb, gb, kb, vb = inp
        bcum = jnp.cumsum(gb, axis=2)
        inter = jnp.einsum('bhtk,bhkv->bhtv', qb * jnp.exp(bcum), state)
        diff = jnp.where(causal[None, None, :, :, None],
                         bcum[:, :, :, None, :] - bcum[:, :, None, :, :], -jnp.inf)
        scores = jnp.einsum('bhtk,bhsk,bhtsk->bhts', qb, kb, jnp.exp(diff))
        intra = jnp.einsum('bhts,bhsv->bhtv', scores, vb)
        last = bcum[:, :, -1]
        k_dec = kb * jnp.exp(last[:, :, None, :] - bcum)
        new_state = state * jnp.exp(last)[..., None] + jnp.einsum('bhsk,bhsv->bhkv', k_dec, vb)
        return new_state, inter + intra

    s0 = jnp.zeros((B, H, DK, DV), jnp.float32)
    _, out = lax.scan(step, s0, (qc, gc, kc, vc))
    return out.transpose(1, 0, 3, 2, 4).reshape(B, S, H, DV)


def _hgrn2_layer(h, w_in, lb, o_g, w_out):
    B, S, _ = h.shape
    proj = h @ w_in
    q, fl, i, z = jnp.split(proj, [HG_FDIM, 2 * HG_FDIM, 2 * HG_FDIM + HG_WIDTH], axis=-1)
    fl = fl.astype(jnp.float32)
    log_f = jnp.logaddexp(jnp.log(lb), jnp.log1p(-lb) + jax.nn.log_sigmoid(fl))
    k = (1.0 - lb) * jax.nn.sigmoid(-fl)
    shp = (B, S, HG_HEADS, HG_DK)
    o = _hgrn2_scan(q.reshape(shp), log_f.reshape(shp), k.reshape(shp),
                    i.reshape(B, S, HG_HEADS, HG_DV))
    o = _rms(o, o_g.reshape(HG_HEADS, HG_DV)).astype(h.dtype)
    o = o.reshape(B, S, HG_WIDTH) * jax.nn.silu(z)
    return o @ w_out


def _fox_attn(q, k, v, log_f):
    B, S, H, D = q.shape
    cum = jnp.cumsum(log_f, axis=1).transpose(0, 2, 1)
    qh, kh, vh = (t.transpose(0, 2, 1, 3) for t in (q, k, v))
    scale = D ** -0.5
    outs = []
    for blk in range(S // Q_BLOCK):
        q0, q1 = blk * Q_BLOCK, (blk + 1) * Q_BLOCK
        s = jnp.einsum('bhtd,bhsd->bhts', qh[:, :, q0:q1], kh[:, :, :q1],
                       preferred_element_type=jnp.float32) * scale
        s = s + cum[:, :, q0:q1, None] - cum[:, :, None, :q1]
        tpos = jnp.arange(q0, q1)
        spos = jnp.arange(q1)
        s = jnp.where(spos[None, :] <= tpos[:, None], s, -jnp.inf)
        p = jax.nn.softmax(s, axis=-1)
        outs.append(jnp.einsum('bhts,bhsd->bhtd', p.astype(vh.dtype), vh[:, :, :q1]))
    return jnp.concatenate(outs, axis=2).transpose(0, 2, 1, 3)


def _fox_layer(h, w_in, b_f, q_g, k_g, w_out):
    B, S, _ = h.shape
    proj = h @ w_in
    q, k, v, z, fl = jnp.split(proj, [FX_WIDTH, 2 * FX_WIDTH, 3 * FX_WIDTH, 4 * FX_WIDTH], axis=-1)
    shp = (B, S, FX_HEADS, FX_HD)
    q = _rms(q.reshape(shp), q_g)
    k = _rms(k.reshape(shp), k_g)
    log_f = jax.nn.log_sigmoid(fl.astype(jnp.float32) + b_f.astype(jnp.float32))
    o = _fox_attn(q, k, v.reshape(shp), log_f)
    o = o.reshape(B, S, FX_WIDTH) * jax.nn.silu(z)
    return o @ w_out


def setup_inputs(seed: int = 0) -> dict:
    key = jax.random.key(seed)
    ks = jax.random.split(key, 16)
    f32 = jnp.float32
    D = D_MODEL
    return {
        "x": jax.random.normal(ks[0], (BATCH, SEQ, D), f32),
        "c": jax.random.normal(ks[1], (BATCH, D), f32),
        "norm_g": 1.0 + 0.05 * jax.random.normal(ks[2], (DEPTH, D), f32),
        "ada_w": 0.5 * D ** -0.5 * jax.random.normal(ks[3], (DEPTH, D, 3 * D), f32),
        "ada_b": 0.02 * jax.random.normal(ks[4], (DEPTH, 3 * D), f32),
        "hg_lb_logits": jax.random.normal(ks[5], (N_HG_LAYERS, HG_FDIM), f32),
        "hg_w_in": D ** -0.5 * jax.random.normal(ks[6], (N_HG_LAYERS, D, HG_IN), f32),
        "hg_o_g": 1.0 + 0.05 * jax.random.normal(ks[7], (N_HG_LAYERS, HG_WIDTH), f32),
        "hg_w_out": HG_WIDTH ** -0.5 * jax.random.normal(ks[8], (N_HG_LAYERS, HG_WIDTH, D), f32),
        "fx_w_in": D ** -0.5 * jax.random.normal(ks[9], (N_FX_LAYERS, D, FX_IN), f32),
        "fx_b_f": jax.random.uniform(ks[10], (N_FX_LAYERS, FX_HEADS), f32, 1.0, 5.0),
        "fx_q_g": 1.0 + 0.05 * jax.random.normal(ks[11], (N_FX_LAYERS, FX_HD), f32),
        "fx_k_g": 1.0 + 0.05 * jax.random.normal(ks[12], (N_FX_LAYERS, FX_HD), f32),
        "fx_w_out": FX_WIDTH ** -0.5 * jax.random.normal(ks[13], (N_FX_LAYERS, FX_WIDTH, D), f32),
    }


def reference(x, c, norm_g, ada_w, ada_b, hg_lb_logits, hg_w_in, hg_o_g, hg_w_out,
              fx_w_in, fx_b_f, fx_q_g, fx_k_g, fx_w_out):
    lb_all = jnp.cumsum(jax.nn.softmax(hg_lb_logits.astype(jnp.float32), axis=0), axis=0)
    lb_all = lb_all - lb_all[0:1]
    c_act = jax.nn.silu(c)
    for layer in range(DEPTH):
        mod = c_act @ ada_w[layer] + ada_b[layer]
        shift, scale, gate = jnp.split(mod, 3, axis=-1)
        h = _rms(x, norm_g[layer]) * (1.0 + scale[:, None, :]) + shift[:, None, :]
        j = layer // N_MIXERS
        if layer % N_MIXERS == 0:
            y = _hgrn2_layer(h, hg_w_in[j], lb_all[j], hg_o_g[j], hg_w_out[j])
        else:
            y = _fox_layer(h, fx_w_in[j], fx_b_f[j], fx_q_g[j], fx_k_g[j], fx_w_out[j])
        x = x + gate[:, None, :] * y
    return x
```

```python
import functools

import jax
import jax.numpy as jnp
from jax import lax
from jax.experimental import pallas as pl
from jax.experimental.pallas import tpu as pltpu

F32 = jnp.float32
BF16 = jnp.bfloat16

EPS = 1e-6
LANES = 128
HG_DK = 128
HG_CHUNK = 64
HG_SUB = 8
FX_HD = 64
SEQ_TILE = 256
VMEM_LIMIT_BYTES = 56 * 1024 * 1024

_NT = (((1,), (1,)), ((), ()))
_TN = (((0,), (0,)), ((), ()))


def _split3(x):
    hi = x.astype(BF16)
    r1 = x - hi.astype(F32)
    mid = r1.astype(BF16)
    lo = (r1 - mid.astype(F32)).astype(BF16)
    return hi, mid, lo


def _modulated_norm(x, g, scale, shift):
    ms = jnp.mean(x * x, axis=-1, keepdims=True)
    return (x * lax.rsqrt(ms + EPS) * g) * (1.0 + scale) + shift


def _silu(x):
    return x / (1.0 + jnp.exp(-x))


def _log_sigmoid(x):
    return jnp.minimum(x, 0.0) - jnp.log1p(jnp.exp(-jnp.abs(x)))


def _adaln_kernel(c_ref, w_ref, b_ref, o_ref):
    c_act = _silu(c_ref[...])
    o_ref[0] = jnp.dot(c_act, w_ref[0], precision=lax.Precision.HIGHEST,
                       preferred_element_type=F32) + b_ref[0]


def _adaln(c, ada_w, ada_b):
    depth, d, n = ada_w.shape
    b = c.shape[0]
    tn = 1024
    return pl.pallas_call(
        _adaln_kernel,
        grid=(depth, n // tn),
        in_specs=[
            pl.BlockSpec((b, d), lambda l, j: (0, 0)),
            pl.BlockSpec((1, d, tn), lambda l, j: (l, 0, j)),
            pl.BlockSpec((1, 1, tn), lambda l, j: (l, 0, j)),
        ],
        out_specs=pl.BlockSpec((1, b, tn), lambda l, j: (l, 0, j)),
        out_shape=jax.ShapeDtypeStruct((depth, b, n), F32),
        compiler_params=pltpu.CompilerParams(
            dimension_semantics=("arbitrary", "arbitrary"), vmem_limit_bytes=VMEM_LIMIT_BYTES),
        name="adaln_mod",
    )(c, ada_w, ada_b.reshape(depth, 1, n))


def _hgrn_chunk(q, lf, k, v, st, tri3, ones_kk, off_masks, diag_masks):
    c = HG_CHUNK
    hi, mid, lo = _split3(lf)
    b = jnp.dot(tri3, jnp.concatenate([hi, mid, lo], axis=0), preferred_element_type=F32)
    blast = b[c - 1:c, :]
    q1 = (q * jnp.exp(b)).astype(BF16)
    kd = (k * jnp.exp(blast - b)).astype(BF16)
    vb = v.astype(BF16)

    o = lax.dot_general(q1, st.astype(BF16), _NT, preferred_element_type=F32)

    a_off = None
    for half, mask in zip((32, 16, 8), off_masks):
        qp, kp = [], []
        for g in range(c // (2 * half)):
            lo_r = slice(2 * half * g, 2 * half * g + half)
            up_r = slice(2 * half * g + half, 2 * half * (g + 1))
            ref = b[2 * half * g + half - 1:2 * half * g + half, :]
            zeros = jnp.zeros((half, HG_DK), F32)
            qp += [zeros, q[up_r] * jnp.exp(b[up_r] - ref)]
            kp += [k[lo_r] * jnp.exp(ref - b[lo_r]), zeros]
        qp = jnp.concatenate(qp, axis=0).astype(BF16)
        kp = jnp.concatenate(kp, axis=0).astype(BF16)
        a_l = lax.dot_general(qp, kp, _NT, preferred_element_type=F32)
        a_l = a_l if mask is None else jnp.where(mask, a_l, 0.0)
        a_off = a_l if a_off is None else a_off + a_l
    o = o + jnp.dot(a_off.astype(BF16), vb, preferred_element_type=F32)

    e_rows = []
    for r in range(c // HG_SUB):
        rs = slice(HG_SUB * r, HG_SUB * (r + 1))
        qr, kr, br = q[rs], k[rs], b[rs]
        for s in range(HG_SUB):
            dec = jnp.exp(jnp.where(diag_masks[s], br - br[s:s + 1, :], -jnp.inf))
            e_rows.append(qr * kr[s:s + 1, :] * dec)
    e_all = jnp.concatenate(e_rows, axis=0).astype(BF16)
    red = jnp.dot(e_all, ones_kk, preferred_element_type=F32)
    o_diag = []
    for r in range(c // HG_SUB):
        vr = v[HG_SUB * r:HG_SUB * (r + 1)]
        acc = None
        for s in range(HG_SUB):
            base = (r * HG_SUB + s) * HG_SUB
            term = red[base:base + HG_SUB, :] * vr[s:s + 1, :]
            acc = term if acc is None else acc + term
        o_diag.append(acc)
    o = o + jnp.concatenate(o_diag, axis=0)

    st_new = st * jnp.exp(blast) + lax.dot_general(vb, kd, _TN, preferred_element_type=F32)
    return o, st_new


def _hgrn_kernel(x_ref, mod_ref, ng_ref, win_ref, lb_ref, og_ref, wout_ref, o_ref,
                 st_ref, q_s, lf_s, k_s, v_s, z_s, oh_s):
    i = pl.program_id(1)
    tm = x_ref.shape[1]
    d = x_ref.shape[2]
    nh = d // HG_DK
    c = HG_CHUNK

    @pl.when(i == 0)
    def _():
        st_ref[...] = jnp.zeros_like(st_ref)

    x = x_ref[0]
    shift, scale, gate = mod_ref[0, 0:1, :], mod_ref[0, 1:2, :], mod_ref[0, 2:3, :]
    h = _modulated_norm(x, ng_ref[...], scale, shift).astype(BF16)

    lb = lb_ref[...]
    l0, l1, oml = jnp.log(lb), jnp.log1p(-lb), 1.0 - lb
    for blk in range(4 * d // 256):
        p = jnp.dot(h, win_ref[:, 256 * blk:256 * (blk + 1)], preferred_element_type=F32)
        sect, hp = divmod(blk, nh // 2)
        for hh in range(2):
            head = 2 * hp + hh
            ph = p[:, HG_DK * hh:HG_DK * (hh + 1)]
            if sect == 0:
                q_s[head] = ph
            elif sect == 1:
                lbh = slice(HG_DK * head, HG_DK * (head + 1))
                e = jnp.exp(-jnp.abs(ph))
                lp = jnp.log1p(e)
                cc = l1[:, lbh] + (jnp.minimum(ph, 0.0) - lp)
                lf_s[head] = jnp.logaddexp(l0[:, lbh], cc)
                k_s[head] = oml[:, lbh] * (jnp.where(ph >= 0.0, e, 1.0) / (1.0 + e))
            elif sect == 2:
                v_s[head] = ph
            else:
                z_s[:, HG_DK * head:HG_DK * (head + 1)] = ph

    ri = lax.broadcasted_iota(jnp.int32, (c, 3 * c), 0)
    ci = lax.broadcasted_iota(jnp.int32, (c, 3 * c), 1)
    tri3 = ((ci % c) <= ri).astype(BF16)
    ones_kk = jnp.ones((HG_DK, HG_DK), BF16)
    ti = lax.broadcasted_iota(jnp.int32, (c, c), 0)
    si = lax.broadcasted_iota(jnp.int32, (c, c), 1)
    off_masks = [None]
    for half in (16, 8):
        off_masks.append(((ti // (2 * half)) == (si // (2 * half)))
                         & ((ti // half) % 2 == 1) & ((si // half) % 2 == 0))
    sub = lax.broadcasted_iota(jnp.int32, (HG_SUB, HG_DK), 0)
    diag_masks = [sub >= s for s in range(HG_SUB)]

    def head_body(hd, carry):
        st = st_ref[hd]
        for ch in range(tm // c):
            rows = slice(c * ch, c * (ch + 1))
            o_c, st = _hgrn_chunk(q_s[hd, rows, :], lf_s[hd, rows, :], k_s[hd, rows, :],
                                  v_s[hd, rows, :], st, tri3, ones_kk, off_masks, diag_masks)
            oh_s[hd, rows, :] = o_c
        st_ref[hd] = st
        return carry

    lax.fori_loop(0, nh, head_body, 0)

    og = og_ref[...]
    outs = []
    for head in range(nh):
        oh = oh_s[head]
        ms = jnp.mean(oh * oh, axis=-1, keepdims=True)
        outs.append(oh * lax.rsqrt(ms + EPS) * og[:, HG_DK * head:HG_DK * (head + 1)])
    on = jnp.concatenate(outs, axis=1) * _silu(z_s[...])
    y = jnp.dot(on.astype(BF16), wout_ref[...], preferred_element_type=F32)
    o_ref[0] = x + gate * y


def _const_spec(shape):
    return pl.BlockSpec(shape, lambda b, i: (0,) * len(shape), pipeline_mode=pl.Buffered(1))


def _hgrn_layer(x, mod, norm_g, w_in, lb, o_g, w_out):
    bsz, seq, d = x.shape
    tm = SEQ_TILE
    nh = d // HG_DK
    slab = pltpu.VMEM((nh, tm, HG_DK), F32)
    return pl.pallas_call(
        _hgrn_kernel,
        grid=(bsz, seq // tm),
        in_specs=[
            pl.BlockSpec((1, tm, d), lambda b, i: (b, i, 0)),
            pl.BlockSpec((1, 3, d), lambda b, i: (b, 0, 0)),
            _const_spec((1, d)),
            _const_spec((d, 4 * d)),
            _const_spec((1, d)),
            _const_spec((1, d)),
            _const_spec((d, d)),
        ],
        out_specs=pl.BlockSpec((1, tm, d), lambda b, i: (b, i, 0)),
        out_shape=jax.ShapeDtypeStruct(x.shape, F32),
        scratch_shapes=[
            pltpu.VMEM((nh, HG_DK, HG_DK), F32),
            slab, slab, slab, slab,
            pltpu.VMEM((tm, d), F32),
            slab,
        ],
        compiler_params=pltpu.CompilerParams(
            dimension_semantics=("arbitrary", "arbitrary"), vmem_limit_bytes=VMEM_LIMIT_BYTES),
        name="hgrn_layer",
    )(x, mod, norm_g.reshape(1, d), w_in.astype(BF16), lb.reshape(1, d), o_g.reshape(1, d),
      w_out.astype(BF16))


def _fox_kernel(x_ref, mod_ref, ng_ref, wq_ref, wkt_ref, wv_ref, wz_ref, wft_ref, bf_ref,
                qg_ref, kg_ref, seg_ref, segt_ref, tri3_ref, wout_ref, o_ref,
                kt_s, v_s, cum_s, carry_s, h_s, q_s, oh_s):
    i = pl.program_id(1)
    tm = x_ref.shape[1]
    d = x_ref.shape[2]
    nh = d // FX_HD
    npair = nh // 2
    tk = tm
    col0 = pl.multiple_of(i * tm, tm)

    x = x_ref[0]
    shift, scale, gate = mod_ref[0, 0:1, :], mod_ref[0, 1:2, :], mod_ref[0, 2:3, :]
    h = _modulated_norm(x, ng_ref[...], scale, shift).astype(BF16)
    h_s[...] = h

    q = jnp.dot(h, wq_ref[...], preferred_element_type=F32)
    ss = jnp.dot((q * q).astype(BF16), seg_ref[...], preferred_element_type=F32)
    r = lax.rsqrt(ss * (1.0 / FX_HD) + EPS)
    r_hi = r.astype(BF16)
    r_lo = (r - r_hi.astype(F32)).astype(BF16)
    rr = (jnp.dot(r_hi, segt_ref[...], preferred_element_type=F32)
          + jnp.dot(r_lo, segt_ref[...], preferred_element_type=F32))
    qn = (q * rr * (qg_ref[...] * (FX_HD ** -0.5))).astype(BF16)
    for p in range(npair):
        q_s[p] = qn[:, LANES * p:LANES * (p + 1)]

    kt = lax.dot_general(wkt_ref[...], h, _NT, preferred_element_type=F32)
    kt3 = kt.reshape(nh, FX_HD, tm)
    kms = jnp.mean(kt3 * kt3, axis=1, keepdims=True)
    ktn = (kt3 * lax.rsqrt(kms + EPS) * kg_ref[...].reshape(nh, FX_HD, 1)).astype(BF16)
    kt_s[:, :, pl.ds(col0, tm)] = ktn.reshape(npair, LANES, tm)

    v = jnp.dot(h, wv_ref[...], preferred_element_type=F32).astype(BF16)
    for p in range(npair):
        v_s[p, pl.ds(col0, tm), :] = v[:, LANES * p:LANES * (p + 1)]

    flt = lax.dot_general(wft_ref[...], h, _NT, preferred_element_type=F32)
    lf = _log_sigmoid(flt + bf_ref[...])
    hi, mid, lo = _split3(lf)
    cum = jnp.dot(jnp.concatenate([hi, mid, lo], axis=1), tri3_ref[...],
                  preferred_element_type=F32)
    prev = jnp.where(i == 0, 0.0, carry_s[:, 0:1])
    cum = cum + prev
    carry_s[...] = jnp.broadcast_to(cum[:, tm - 1:tm], carry_s.shape)
    cum_s[:, pl.ds(col0, tm)] = cum

    lane = lax.broadcasted_iota(jnp.int32, (tm, LANES), 1)
    row_i = lax.broadcasted_iota(jnp.int32, (tm, tk), 0)
    col_i = lax.broadcasted_iota(jnp.int32, (tm, tk), 1)
    causal = col_i <= row_i

    def pair_body(p, carry):
        qp = q_s[p]
        qz = [jnp.where(lane < FX_HD, qp, jnp.zeros_like(qp)),
              jnp.where(lane >= FX_HD, qp, jnp.zeros_like(qp))]

        def tile_update(j, state, masked):
            cols = pl.ds(pl.multiple_of(j * tk, tk), tk)
            ktile = kt_s[p, :, cols]
            vtile = v_s[p, cols, :]
            new = []
            for hh in range(2):
                m, l, acc = state[hh]
                s = jnp.dot(qz[hh], ktile, preferred_element_type=F32)
                s = s - cum_s[pl.ds(2 * p + hh, 1), cols]
                if masked:
                    s = jnp.where(causal, s, -jnp.inf)
                m_new = jnp.maximum(m, jnp.max(s, axis=-1, keepdims=True))
                alpha = jnp.exp(m - m_new)
                pr = jnp.exp(s - m_new)
                l = alpha * l + jnp.sum(pr, axis=-1, keepdims=True)
                acc = alpha * acc + jnp.dot(pr.astype(BF16), vtile, preferred_element_type=F32)
                new.append((m_new, l, acc))
            return tuple(new)

        init = tuple((jnp.full((tm, 1), -jnp.inf, F32), jnp.zeros((tm, 1), F32),
                      jnp.zeros((tm, LANES), F32)) for _ in range(2))
        state = lax.fori_loop(0, i, lambda j, st: tile_update(j, st, False), init)
        state = tile_update(i, state, True)
        (_, l0, a0), (_, l1, a1) = state
        oh_s[p] = jnp.where(lane < FX_HD, a0 / l0, a1 / l1)
        return carry

    lax.fori_loop(0, npair, pair_body, 0)

    z = jnp.dot(h_s[...], wz_ref[...], preferred_element_type=F32)
    o = jnp.concatenate([oh_s[p] for p in range(npair)], axis=1) * _silu(z)
    y = jnp.dot(o.astype(BF16), wout_ref[...], preferred_element_type=F32)
    o_ref[0] = x + gate * y


def _fox_layer(x, mod, norm_g, w_in, b_f, q_g, k_g, w_out):
    bsz, seq, d = x.shape
    tm = SEQ_TILE
    nh = d // FX_HD
    npair = nh // 2
    wq, wk, wv, wz, wf = (w_in[:, :d], w_in[:, d:2 * d], w_in[:, 2 * d:3 * d],
                          w_in[:, 3 * d:4 * d], w_in[:, 4 * d:])
    head_of = jnp.arange(d) // FX_HD
    seg = (head_of[:, None] == jnp.arange(LANES)[None, :]).astype(BF16)
    tri = (jnp.arange(tm)[:, None] <= jnp.arange(tm)[None, :]).astype(BF16)
    tri3 = jnp.concatenate([tri, tri, tri], axis=0)
    return pl.pallas_call(
        _fox_kernel,
        grid=(bsz, seq // tm),
        in_specs=[
            pl.BlockSpec((1, tm, d), lambda b, i: (b, i, 0)),
            pl.BlockSpec((1, 3, d), lambda b, i: (b, 0, 0)),
            _const_spec((1, d)),
            _const_spec((d, d)),
            _const_spec((d, d)),
            _const_spec((d, d)),
            _const_spec((d, d)),
            _const_spec((nh, d)),
            _const_spec((nh, 1)),
            _const_spec((1, d)),
            _const_spec((d, 1)),
            _const_spec((d, LANES)),
            _const_spec((LANES, d)),
            _const_spec((3 * tm, tm)),
            _const_spec((d, d)),
        ],
        out_specs=pl.BlockSpec((1, tm, d), lambda b, i: (b, i, 0)),
        out_shape=jax.ShapeDtypeStruct(x.shape, F32),
        scratch_shapes=[
            pltpu.VMEM((npair, LANES, seq), BF16),
            pltpu.VMEM((npair, seq, LANES), BF16),
            pltpu.VMEM((nh, seq), F32),
            pltpu.VMEM((nh, LANES), F32),
            pltpu.VMEM((tm, d), BF16),
            pltpu.VMEM((npair, tm, LANES), BF16),
            pltpu.VMEM((npair, tm, LANES), F32),
        ],
        compiler_params=pltpu.CompilerParams(
            dimension_semantics=("arbitrary", "arbitrary"), vmem_limit_bytes=VMEM_LIMIT_BYTES),
        name="fox_layer",
    )(x, mod, norm_g.reshape(1, d), wq.astype(BF16), wk.T.astype(BF16), wv.astype(BF16),
      wz.astype(BF16), wf.T.astype(BF16), b_f.reshape(nh, 1), jnp.tile(q_g, nh).reshape(1, d),
      jnp.tile(k_g, nh).reshape(d, 1), seg, seg.T, tri3, w_out.astype(BF16))


def kernel(x, c, norm_g, ada_w, ada_b, hg_lb_logits, hg_w_in, hg_o_g, hg_w_out,
           fx_w_in, fx_b_f, fx_q_g, fx_k_g, fx_w_out):
    depth = norm_g.shape[0]
    bsz, _, d = x.shape
    lb_all = jnp.cumsum(jax.nn.softmax(hg_lb_logits.astype(F32), axis=0), axis=0)
    lb_all = lb_all - lb_all[0:1]
    mod = _adaln(c, ada_w, ada_b).reshape(depth, bsz, 3, d)
    for layer in range(depth):
        j = layer // 2
        if layer % 2 == 0:
            x = _hgrn_layer(x, mod[layer], norm_g[layer], hg_w_in[j], lb_all[j], hg_o_g[j],
                            hg_w_out[j])
        else:
            x = _fox_layer(x, mod[layer], norm_g[layer], fx_w_in[j], fx_b_f[j], fx_q_g[j],
                           fx_k_g[j], fx_w_out[j])
    return x
```

```python
import functools

import jax
import jax.numpy as jnp
from jax import lax
from jax.experimental import pallas as pl
from jax.experimental.pallas import tpu as pltpu

F32 = jnp.float32
BF16 = jnp.bfloat16

EPS = 1e-6
LANES = 128
HG_DK = 128
HG_CHUNK = 64
HG_SUB = 8
FX_HD = 64
SEQ_TILE = 256
VMEM_LIMIT_BYTES = 56 * 1024 * 1024

_NT = (((1,), (1,)), ((), ()))
_TN = (((0,), (0,)), ((), ()))


def _split3(x):
    hi = x.astype(BF16)
    r1 = x - hi.astype(F32)
    mid = r1.astype(BF16)
    lo = (r1 - mid.astype(F32)).astype(BF16)
    return hi, mid, lo


def _modulated_norm(x, g, scale, shift):
    ms = jnp.mean(x * x, axis=-1, keepdims=True)
    return (x * lax.rsqrt(ms + EPS) * g) * (1.0 + scale) + shift


def _silu(x):
    return x / (1.0 + jnp.exp(-x))


def _log_sigmoid(x):
    return jnp.minimum(x, 0.0) - jnp.log1p(jnp.exp(-jnp.abs(x)))


def _adaln_kernel(c_ref, w_ref, b_ref, o_ref):
    c_act = _silu(c_ref[...])
    o_ref[0] = jnp.dot(c_act, w_ref[0], precision=lax.Precision.HIGHEST,
                       preferred_element_type=F32) + b_ref[0]


def _adaln(c, ada_w, ada_b):
    depth, d, n = ada_w.shape
    b = c.shape[0]
    tn = 1024
    return pl.pallas_call(
        _adaln_kernel,
        grid=(depth, n // tn),
        in_specs=[
            pl.BlockSpec((b, d), lambda l, j: (0, 0)),
            pl.BlockSpec((1, d, tn), lambda l, j: (l, 0, j)),
            pl.BlockSpec((1, 1, tn), lambda l, j: (l, 0, j)),
        ],
        out_specs=pl.BlockSpec((1, b, tn), lambda l, j: (l, 0, j)),
        out_shape=jax.ShapeDtypeStruct((depth, b, n), F32),
        compiler_params=pltpu.CompilerParams(
            dimension_semantics=("arbitrary", "arbitrary"), vmem_limit_bytes=VMEM_LIMIT_BYTES),
        name="adaln_mod",
    )(c, ada_w, ada_b.reshape(depth, 1, n))


def _hgrn_chunk(q, lf, k, v, st, tri3, ones_kk, off_masks, diag_masks):
    c = HG_CHUNK
    hi, mid, lo = _split3(lf)
    b = jnp.dot(tri3, jnp.concatenate([hi, mid, lo], axis=0), preferred_element_type=F32)
    blast = b[c - 1:c, :]
    q1 = (q * jnp.exp(b)).astype(BF16)
    kd = (k * jnp.exp(blast - b)).astype(BF16)
    vb = v.astype(BF16)

    o = lax.dot_general(q1, st.astype(BF16), _NT, preferred_element_type=F32)

    a_off = None
    for half, mask in zip((32, 16, 8), off_masks):
        qp, kp = [], []
        for g in range(c // (2 * half)):
            lo_r = slice(2 * half * g, 2 * half * g + half)
            up_r = slice(2 * half * g + half, 2 * half * (g + 1))
            ref = b[2 * half * g + half - 1:2 * half * g + half, :]
            zeros = jnp.zeros((half, HG_DK), F32)
            qp += [zeros, q[up_r] * jnp.exp(b[up_r] - ref)]
            kp += [k[lo_r] * jnp.exp(ref - b[lo_r]), zeros]
        qp = jnp.concatenate(qp, axis=0).astype(BF16)
        kp = jnp.concatenate(kp, axis=0).astype(BF16)
        a_l = lax.dot_general(qp, kp, _NT, preferred_element_type=F32)
        a_l = a_l if mask is None else jnp.where(mask, a_l, 0.0)
        a_off = a_l if a_off is None else a_off + a_l
    o = o + jnp.dot(a_off.astype(BF16), vb, preferred_element_type=F32)

    e_rows = []
    for r in range(c // HG_SUB):
        rs = slice(HG_SUB * r, HG_SUB * (r + 1))
        qr, kr, br = q[rs], k[rs], b[rs]
        for s in range(HG_SUB):
            dec = jnp.exp(jnp.where(diag_masks[s], br - br[s:s + 1, :], -jnp.inf))
            e_rows.append(qr * kr[s:s + 1, :] * dec)
    e_all = jnp.concatenate(e_rows, axis=0).astype(BF16)
    red = jnp.dot(e_all, ones_kk, preferred_element_type=F32)
    o_diag = []
    for r in range(c // HG_SUB):
        vr = v[HG_SUB * r:HG_SUB * (r + 1)]
        acc = None
        for s in range(HG_SUB):
            base = (r * HG_SUB + s) * HG_SUB
            term = red[base:base + HG_SUB, :] * vr[s:s + 1, :]
            acc = term if acc is None else acc + term
        o_diag.append(acc)
    o = o + jnp.concatenate(o_diag, axis=0)

    st_new = st * jnp.exp(blast) + lax.dot_general(vb, kd, _TN, preferred_element_type=F32)
    return o, st_new


def _hgrn_kernel(x_ref, mod_ref, ng_ref, win_ref, lb_ref, og_ref, wout_ref, o_ref,
                 st_ref, q_s, lf_s, k_s, v_s, z_s, oh_s):
    i = pl.program_id(1)
    tm = x_ref.shape[1]
    d = x_ref.shape[2]
    nh = d // HG_DK
    c = HG_CHUNK

    @pl.when(i == 0)
    def _():
        st_ref[...] = jnp.zeros_like(st_ref)

    x = x_ref[0]
    shift, scale, gate = mod_ref[0, 0:1, :], mod_ref[0, 1:2, :], mod_ref[0, 2:3, :]
    h = _modulated_norm(x, ng_ref[...], scale, shift).astype(BF16)

    lb = lb_ref[...]
    l0, l1, oml = jnp.log(lb), jnp.log1p(-lb), 1.0 - lb
    for blk in range(4 * d // 256):
        p = jnp.dot(h, win_ref[:, 256 * blk:256 * (blk + 1)], preferred_element_type=F32)
        sect, hp = divmod(blk, nh // 2)
        for hh in range(2):
            head = 2 * hp + hh
            ph = p[:, HG_DK * hh:HG_DK * (hh + 1)]
            if sect == 0:
                q_s[head] = ph
            elif sect == 1:
                lbh = slice(HG_DK * head, HG_DK * (head + 1))
                e = jnp.exp(-jnp.abs(ph))
                lp = jnp.log1p(e)
                cc = l1[:, lbh] + (jnp.minimum(ph, 0.0) - lp)
                lf_s[head] = jnp.logaddexp(l0[:, lbh], cc)
                k_s[head] = oml[:, lbh] * (jnp.where(ph >= 0.0, e, 1.0) / (1.0 + e))
            elif sect == 2:
                v_s[head] = ph
            else:
                z_s[:, HG_DK * head:HG_DK * (head + 1)] = ph

    ri = lax.broadcasted_iota(jnp.int32, (c, 3 * c), 0)
    ci = lax.broadcasted_iota(jnp.int32, (c, 3 * c), 1)
    tri3 = ((ci % c) <= ri).astype(BF16)
    ones_kk = jnp.ones((HG_DK, HG_DK), BF16)
    ti = lax.broadcasted_iota(jnp.int32, (c, c), 0)
    si = lax.broadcasted_iota(jnp.int32, (c, c), 1)
    off_masks = [None]
    for half in (16, 8):
        off_masks.append(((ti // (2 * half)) == (si // (2 * half)))
                         & ((ti // half) % 2 == 1) & ((si // half) % 2 == 0))
    sub = lax.broadcasted_iota(jnp.int32, (HG_SUB, HG_DK), 0)
    diag_masks = [sub >= s for s in range(HG_SUB)]

    def head_body(hd, carry):
        st = st_ref[hd]
        for ch in range(tm // c):
            rows = slice(c * ch, c * (ch + 1))
            o_c, st = _hgrn_chunk(q_s[hd, rows, :], lf_s[hd, rows, :], k_s[hd, rows, :],
                                  v_s[hd, rows, :], st, tri3, ones_kk, off_masks, diag_masks)
            oh_s[hd, rows, :] = o_c
        st_ref[hd] = st
        return carry

    lax.fori_loop(0, nh, head_body, 0)

    og = og_ref[...]
    outs = []
    for head in range(nh):
        oh = oh_s[head]
        ms = jnp.mean(oh * oh, axis=-1, keepdims=True)
        outs.append(oh * lax.rsqrt(ms + EPS) * og[:, HG_DK * head:HG_DK * (head + 1)])
    on = jnp.concatenate(outs, axis=1) * _silu(z_s[...])
    y = jnp.dot(on.astype(BF16), wout_ref[...], preferred_element_type=F32)
    o_ref[0] = x + gate * y


def _const_spec(shape):
    return pl.BlockSpec(shape, lambda b, i: (0,) * len(shape), pipeline_mode=pl.Buffered(1))


def _hgrn_layer(x, mod, norm_g, w_in, lb, o_g, w_out):
    bsz, seq, d = x.shape
    tm = SEQ_TILE
    nh = d // HG_DK
    slab = pltpu.VMEM((nh, tm, HG_DK), F32)
    return pl.pallas_call(
        _hgrn_kernel,
        grid=(bsz, seq // tm),
        in_specs=[
            pl.BlockSpec((1, tm, d), lambda b, i: (b, i, 0)),
            pl.BlockSpec((1, 3, d), lambda b, i: (b, 0, 0)),
            _const_spec((1, d)),
            _const_spec((d, 4 * d)),
            _const_spec((1, d)),
            _const_spec((1, d)),
            _const_spec((d, d)),
        ],
        out_specs=pl.BlockSpec((1, tm, d), lambda b, i: (b, i, 0)),
        out_shape=jax.ShapeDtypeStruct(x.shape, F32),
        scratch_shapes=[
            pltpu.VMEM((nh, HG_DK, HG_DK), F32),
            slab, slab, slab, slab,
            pltpu.VMEM((tm, d), F32),
            slab,
        ],
        compiler_params=pltpu.CompilerParams(
            dimension_semantics=("arbitrary", "arbitrary"), vmem_limit_bytes=VMEM_LIMIT_BYTES),
        name="hgrn_layer",
    )(x, mod, norm_g.reshape(1, d), w_in.astype(BF16), lb.reshape(1, d), o_g.reshape(1, d),
      w_out.astype(BF16))


def _fox_kernel(x_ref, mod_ref, ng_ref, wqt_ref, wk_ref, wvt_ref, wzt_ref, wf_ref, bf_ref,
                qg_ref, kg_ref, seg_ref, segt_ref, tri3_ref, negsel_ref, wout_ref, o_ref,
                k_s, cp_s, vt_s, carry_s, h_s, wq_s, m_s, l_s, acc_s):
    b = pl.program_id(0)
    i = pl.program_id(1)
    tm = x_ref.shape[1]
    d = x_ref.shape[2]
    nh = d // FX_HD
    npair = nh // 2
    tk = tm
    row0 = pl.multiple_of(i * tm, tm)

    @pl.when((b == 0) & (i == 0))
    def _():
        wq_s[:, LANES:, :] = negsel_ref[...]

    x = x_ref[0]
    shift, scale, gate = mod_ref[0, 0:1, :], mod_ref[0, 1:2, :], mod_ref[0, 2:3, :]
    h = _modulated_norm(x, ng_ref[...], scale, shift).astype(BF16)
    h_s[...] = h

    qt3 = lax.dot_general(wqt_ref[...], h, _NT, preferred_element_type=F32).reshape(nh, FX_HD, tm)
    qms = jnp.mean(qt3 * qt3, axis=1, keepdims=True)
    qgain = (qg_ref[...] * (FX_HD ** -0.5)).reshape(nh, FX_HD, 1)
    qtn = (qt3 * lax.rsqrt(qms + EPS) * qgain).astype(BF16)
    zeros = jnp.zeros((FX_HD, tm), BF16)
    for hd in range(nh):
        halves = [qtn[hd], zeros] if hd % 2 == 0 else [zeros, qtn[hd]]
        wq_s[hd, :LANES, :] = jnp.concatenate(halves, axis=0)

    k = jnp.dot(h, wk_ref[...], preferred_element_type=F32)
    ss = jnp.dot((k * k).astype(BF16), seg_ref[...], preferred_element_type=F32)
    r = lax.rsqrt(ss * (1.0 / FX_HD) + EPS)
    r_hi = r.astype(BF16)
    r_lo = (r - r_hi.astype(F32)).astype(BF16)
    rr = (jnp.dot(r_hi, segt_ref[...], preferred_element_type=F32)
          + jnp.dot(r_lo, segt_ref[...], preferred_element_type=F32))
    kn = (k * rr * kg_ref[...]).astype(BF16)
    for p in range(npair):
        k_s[p, pl.ds(row0, tm), :] = kn[:, LANES * p:LANES * (p + 1)]

    vt_s[:, pl.ds(row0, tm)] = lax.dot_general(
        wvt_ref[...], h, _NT, preferred_element_type=F32).astype(BF16)

    lane = lax.broadcasted_iota(jnp.int32, (tm, LANES), 1)
    fl = jnp.dot(h, wf_ref[...], preferred_element_type=F32)
    lf = jnp.where(lane < nh, _log_sigmoid(fl + bf_ref[...]), 0.0)
    hi, mid, lo = _split3(lf)
    cum = jnp.dot(tri3_ref[...], jnp.concatenate([hi, mid, lo], axis=0),
                  preferred_element_type=F32)
    cum = cum + jnp.where(i == 0, 0.0, carry_s[0:1, :])
    carry_s[...] = jnp.broadcast_to(cum[tm - 1:tm, :], carry_s.shape)
    chi, cmid, clo = _split3(cum)
    cp = (chi.astype(F32) + pltpu.roll(cmid.astype(F32), nh, axis=1)
          + pltpu.roll(clo.astype(F32), 2 * nh, axis=1))
    cp_s[pl.ds(row0, tm), :] = cp.astype(BF16)

    m_s[...] = jnp.full(m_s.shape, -jnp.inf, F32)
    l_s[...] = jnp.zeros_like(l_s)
    acc_s[...] = jnp.zeros_like(acc_s)
    key_i = lax.broadcasted_iota(jnp.int32, (tk, tm), 0)
    qry_i = lax.broadcasted_iota(jnp.int32, (tk, tm), 1)
    causal = key_i <= qry_i

    def tile_update(j, masked):
        rows = pl.ds(pl.multiple_of(j * tk, tk), tk)
        cpt = cp_s[rows, :]
        scores = []
        for p in range(npair):
            lhs = jnp.concatenate([k_s[p, rows, :], cpt], axis=1)
            for hh in range(2):
                st = jnp.dot(lhs, wq_s[2 * p + hh], preferred_element_type=F32)
                scores.append(jnp.where(causal, st, -jnp.inf) if masked else st)
        probs, alphas = [], []
        for hd in range(nh):
            st = scores[hd].reshape(tk // 8, 8, tm)
            m_old = m_s[hd]
            mx = jnp.max(st, axis=0)
            mx = jnp.broadcast_to(jnp.max(mx, axis=0, keepdims=True), (8, tm))
            m_new = jnp.maximum(m_old, mx)
            alpha = jnp.exp(m_old - m_new)
            pr = jnp.exp(st - m_new)
            m_s[hd] = m_new
            l_s[hd] = alpha * l_s[hd] + jnp.sum(pr, axis=0)
            probs.append(pr.reshape(tk, tm).astype(BF16))
            alphas.append(alpha)
        for hd in range(nh):
            hs = slice(FX_HD * hd, FX_HD * (hd + 1))
            pv = jnp.dot(vt_s[hs, rows], probs[hd], preferred_element_type=F32)
            acc_s[hs, :] = jnp.tile(alphas[hd], (FX_HD // 8, 1)) * acc_s[hs, :] + pv

    def full_tile(j, carry):
        tile_update(j, False)
        return carry

    lax.fori_loop(0, i, full_tile, 0)
    tile_update(i, True)

    zt = lax.dot_general(wzt_ref[...], h_s[...], _NT, preferred_element_type=F32)
    inv_l = 1.0 / jnp.sum(l_s[...], axis=1, keepdims=True)
    ot = jnp.concatenate(
        [acc_s[FX_HD * hd:FX_HD * (hd + 1), :] * inv_l[hd] for hd in range(nh)], axis=0)
    on = (ot * _silu(zt)).astype(BF16)
    y = lax.dot_general(on, wout_ref[...], _TN, preferred_element_type=F32)
    o_ref[0] = x + gate * y


def _fox_layer(x, mod, norm_g, w_in, b_f, q_g, k_g, w_out):
    bsz, seq, d = x.shape
    tm = SEQ_TILE
    nh = d // FX_HD
    npair = nh // 2
    wq, wk, wv, wz, wf = (w_in[:, :d], w_in[:, d:2 * d], w_in[:, 2 * d:3 * d],
                          w_in[:, 3 * d:4 * d], w_in[:, 4 * d:])
    head_of = jnp.arange(d) // FX_HD
    seg = (head_of[:, None] == jnp.arange(LANES)[None, :]).astype(BF16)
    tri = (jnp.arange(tm)[None, :] <= jnp.arange(tm)[:, None]).astype(BF16)
    tri3 = jnp.concatenate([tri, tri, tri], axis=1)
    sel_row = jnp.arange(LANES)[None, :, None]
    sel_head = jnp.arange(nh)[:, None, None]
    negsel = -((sel_row % nh == sel_head) & (sel_row < 3 * nh)).astype(BF16)
    negsel = jnp.broadcast_to(negsel, (nh, LANES, tm))
    wf_pad = jnp.pad(wf, ((0, 0), (0, LANES - nh))).astype(BF16)
    bf_pad = jnp.pad(b_f, (0, LANES - nh)).reshape(1, LANES)
    return pl.pallas_call(
        _fox_kernel,
        grid=(bsz, seq // tm),
        in_specs=[
            pl.BlockSpec((1, tm, d), lambda b, i: (b, i, 0)),
            pl.BlockSpec((1, 3, d), lambda b, i: (b, 0, 0)),
            _const_spec((1, d)),
            _const_spec((d, d)),
            _const_spec((d, d)),
            _const_spec((d, d)),
            _const_spec((d, d)),
            _const_spec((d, LANES)),
            _const_spec((1, LANES)),
            _const_spec((d, 1)),
            _const_spec((1, d)),
            _const_spec((d, LANES)),
            _const_spec((LANES, d)),
            _const_spec((tm, 3 * tm)),
            _const_spec((nh, LANES, tm)),
            _const_spec((d, d)),
        ],
        out_specs=pl.BlockSpec((1, tm, d), lambda b, i: (b, i, 0)),
        out_shape=jax.ShapeDtypeStruct(x.shape, F32),
        scratch_shapes=[
            pltpu.VMEM((npair, seq, LANES), BF16),
            pltpu.VMEM((seq, LANES), BF16),
            pltpu.VMEM((d, seq), BF16),
            pltpu.VMEM((8, LANES), F32),
            pltpu.VMEM((tm, d), BF16),
            pltpu.VMEM((nh, 2 * LANES, tm), BF16),
            pltpu.VMEM((nh, 8, tm), F32),
            pltpu.VMEM((nh, 8, tm), F32),
            pltpu.VMEM((d, tm), F32),
        ],
        compiler_params=pltpu.CompilerParams(
            dimension_semantics=("arbitrary", "arbitrary"), vmem_limit_bytes=VMEM_LIMIT_BYTES),
        name="fox_layer",
    )(x, mod, norm_g.reshape(1, d), wq.T.astype(BF16), wk.astype(BF16), wv.T.astype(BF16),
      wz.T.astype(BF16), wf_pad, bf_pad, jnp.tile(q_g, nh).reshape(d, 1),
      jnp.tile(k_g, nh).reshape(1, d), seg, seg.T, tri3, negsel, w_out.astype(BF16))


def kernel(x, c, norm_g, ada_w, ada_b, hg_lb_logits, hg_w_in, hg_o_g, hg_w_out,
           fx_w_in, fx_b_f, fx_q_g, fx_k_g, fx_w_out):
    depth = norm_g.shape[0]
    bsz, _, d = x.shape
    lb_all = jnp.cumsum(jax.nn.softmax(hg_lb_logits.astype(F32), axis=0), axis=0)
    lb_all = lb_all - lb_all[0:1]
    mod = _adaln(c, ada_w, ada_b).reshape(depth, bsz, 3, d)
    for layer in range(depth):
        j = layer // 2
        if layer % 2 == 0:
            x = _hgrn_layer(x, mod[layer], norm_g[layer], hg_w_in[j], lb_all[j], hg_o_g[j],
                            hg_w_out[j])
        else:
            x = _fox_layer(x, mod[layer], norm_g[layer], fx_w_in[j], fx_b_f[j], fx_q_g[j],
                           fx_k_g[j], fx_w_out[j])
    return x
```

```python
import functools

import jax
import jax.numpy as jnp
from jax import lax
from jax.experimental import pallas as pl
from jax.experimental.pallas import tpu as pltpu

F32 = jnp.float32
BF16 = jnp.bfloat16

EPS = 1e-6
LOG2E = 1.4426950408889634
LANES = 128
HG_DK = 128
HG_CHUNK = 64
HG_LEVELS = (32, 16, 8, 4, 2, 1, 0)
HG_GROUP = 8
FX_HD = 64
FX_LOOKAHEAD = 4
SEQ_TILE = 256
VMEM_LIMIT_BYTES = 56 * 1024 * 1024

_NT = (((1,), (1,)), ((), ()))
_TN = (((0,), (0,)), ((), ()))


def _split3(x):
    hi = x.astype(BF16)
    r1 = x - hi.astype(F32)
    mid = r1.astype(BF16)
    lo = (r1 - mid.astype(F32)).astype(BF16)
    return hi, mid, lo


def _modulated_norm(x, g, scale, shift):
    ms = jnp.mean(x * x, axis=-1, keepdims=True)
    return (x * lax.rsqrt(ms + EPS) * g) * (1.0 + scale) + shift


def _silu(x):
    return x / (1.0 + jnp.exp(-x))


def _log_sigmoid(x):
    return jnp.minimum(x, 0.0) - jnp.log1p(jnp.exp(-jnp.abs(x)))


def _adaln_kernel(c_ref, w_ref, b_ref, o_ref):
    c_act = _silu(c_ref[...])
    o_ref[0] = jnp.dot(c_act, w_ref[0], precision=lax.Precision.HIGHEST,
                       preferred_element_type=F32) + b_ref[0]


def _adaln(c, ada_w, ada_b):
    depth, d, n = ada_w.shape
    b = c.shape[0]
    tn = 1024
    return pl.pallas_call(
        _adaln_kernel,
        grid=(depth, n // tn),
        in_specs=[
            pl.BlockSpec((b, d), lambda l, j: (0, 0)),
            pl.BlockSpec((1, d, tn), lambda l, j: (l, 0, j)),
            pl.BlockSpec((1, 1, tn), lambda l, j: (l, 0, j)),
        ],
        out_specs=pl.BlockSpec((1, b, tn), lambda l, j: (l, 0, j)),
        out_shape=jax.ShapeDtypeStruct((depth, b, n), F32),
        compiler_params=pltpu.CompilerParams(
            dimension_semantics=("arbitrary", "arbitrary"), vmem_limit_bytes=VMEM_LIMIT_BYTES),
        name="adaln_mod",
    )(c, ada_w, ada_b.reshape(depth, 1, n))


def _hgrn_level_operands(q, k, b, lf, half, sub3):
    c = HG_CHUNK
    if half == 0:
        return q, k
    if half >= 8:
        qp, kp = [], []
        for g in range(c // (2 * half)):
            lo_r = slice(2 * half * g, 2 * half * g + half)
            up_r = slice(2 * half * g + half, 2 * half * (g + 1))
            ref = b[2 * half * g + half - 1:2 * half * g + half, :]
            zeros = jnp.zeros((half, HG_DK), F32)
            qp += [zeros, q[up_r] * jnp.exp2(b[up_r] - ref)]
            kp += [k[lo_r] * jnp.exp2(ref - b[lo_r]), zeros]
        return jnp.concatenate(qp, axis=0), jnp.concatenate(kp, axis=0)
    shape3 = (c // 8, 8, HG_DK)
    q3, k3, b3 = q.reshape(shape3), k.reshape(shape3), b.reshape(shape3)
    if half == 1:
        upper = sub3 % 2 == 1
        qp = jnp.where(upper, q3 * jnp.exp2(lf.reshape(shape3)), 0.0)
        kp = jnp.where(upper, 0.0, k3)
    else:
        if half == 4:
            ref = b3[:, 3:4, :]
        else:
            ref = jnp.where(sub3 < 4, b3[:, 1:2, :], b3[:, 5:6, :])
        upper = sub3 % (2 * half) >= half
        w = jnp.exp2(-jnp.abs(b3 - ref))
        qp = jnp.where(upper, q3 * w, 0.0)
        kp = jnp.where(upper, 0.0, k3 * w)
    return qp.reshape(c, HG_DK), kp.reshape(c, HG_DK)


def _hgrn_heads(qs, lfs, ks, vs, states, tri3, level_masks, sub3):
    c = HG_CHUNK
    n = len(qs)
    per_head = n // len(states)
    bs = [jnp.dot(tri3, jnp.concatenate(_split3(lf), axis=0), preferred_element_type=F32)
          for lf in lfs]
    vbs = [v.astype(BF16) for v in vs]
    a_sums = []
    for i in range(n):
        a_sum = None
        for half, mask in zip(HG_LEVELS, level_masks):
            qp, kp = _hgrn_level_operands(qs[i], ks[i], bs[i], lfs[i], half, sub3)
            a_l = lax.dot_general(qp.astype(BF16), kp.astype(BF16), _NT,
                                  preferred_element_type=F32)
            a_l = a_l if mask is None else jnp.where(mask, a_l, 0.0)
            a_sum = a_l if a_sum is None else a_sum + a_l
        a_sums.append(a_sum)
    o_intra = [jnp.dot(a_sums[i].astype(BF16), vbs[i], preferred_element_type=F32)
               for i in range(n)]
    blasts = [b[c - 1:c, :] for b in bs]
    q1s = [(qs[i] * jnp.exp2(bs[i])).astype(BF16) for i in range(n)]
    kds = [(ks[i] * jnp.exp2(blasts[i] - bs[i])).astype(BF16) for i in range(n)]
    pcs = [lax.dot_general(vbs[i], kds[i], _TN, preferred_element_type=F32) for i in range(n)]
    st_in, st_out = [], []
    for h, st in enumerate(states):
        for i in range(per_head * h, per_head * (h + 1)):
            st_in.append(st)
            st = st * jnp.exp2(blasts[i]) + pcs[i]
        st_out.append(st)
    outs = [o_intra[i] + lax.dot_general(q1s[i], st_in[i].astype(BF16), _NT,
                                         preferred_element_type=F32) for i in range(n)]
    return outs, st_out


def _hgrn_kernel(x_ref, mod_ref, ng_ref, win_ref, lb_ref, og_ref, wout_ref, o_ref,
                 st_ref, q_s, lf_s, k_s, v_s, z_s, oh_s):
    i = pl.program_id(1)
    tm = x_ref.shape[1]
    d = x_ref.shape[2]
    nh = d // HG_DK
    c = HG_CHUNK

    @pl.when(i == 0)
    def _():
        st_ref[...] = jnp.zeros_like(st_ref)

    x = x_ref[0]
    shift, scale, gate = mod_ref[0, 0:1, :], mod_ref[0, 1:2, :], mod_ref[0, 2:3, :]
    h = _modulated_norm(x, ng_ref[...], scale, shift).astype(BF16)

    lb = lb_ref[...]
    l0, l1, oml = jnp.log(lb), jnp.log1p(-lb), 1.0 - lb
    for blk in range(4 * d // 256):
        p = jnp.dot(h, win_ref[:, 256 * blk:256 * (blk + 1)], preferred_element_type=F32)
        sect, hp = divmod(blk, nh // 2)
        for hh in range(2):
            head = 2 * hp + hh
            ph = p[:, HG_DK * hh:HG_DK * (hh + 1)]
            if sect == 0:
                q_s[head] = ph
            elif sect == 1:
                lbh = slice(HG_DK * head, HG_DK * (head + 1))
                e = jnp.exp(-jnp.abs(ph))
                lp = jnp.log1p(e)
                cc = l1[:, lbh] + (jnp.minimum(ph, 0.0) - lp)
                lf_s[head] = jnp.logaddexp(l0[:, lbh], cc) * LOG2E
                k_s[head] = oml[:, lbh] * (jnp.where(ph >= 0.0, e, 1.0) / (1.0 + e))
            elif sect == 2:
                v_s[head] = ph
            else:
                z_s[:, HG_DK * head:HG_DK * (head + 1)] = ph

    ri = lax.broadcasted_iota(jnp.int32, (c, 3 * c), 0)
    ci = lax.broadcasted_iota(jnp.int32, (c, 3 * c), 1)
    tri3 = ((ci % c) <= ri).astype(BF16)
    ti = lax.broadcasted_iota(jnp.int32, (c, c), 0)
    si = lax.broadcasted_iota(jnp.int32, (c, c), 1)
    level_masks = [None if 2 * half == c else
                   (ti == si if half == 0 else (ti // (2 * half)) == (si // (2 * half)))
                   for half in HG_LEVELS]
    sub3 = lax.broadcasted_iota(jnp.int32, (1, 8, HG_DK), 1)
    chunks = [slice(c * ch, c * (ch + 1)) for ch in range(tm // c)]

    def group_body(g, carry):
        heads = [g * HG_GROUP + hh for hh in range(HG_GROUP)]
        flat = [(hd, r) for hd in heads for r in chunks]
        outs, states = _hgrn_heads([q_s[hd, r, :] for hd, r in flat],
                                   [lf_s[hd, r, :] for hd, r in flat],
                                   [k_s[hd, r, :] for hd, r in flat],
                                   [v_s[hd, r, :] for hd, r in flat],
                                   [st_ref[hd] for hd in heads], tri3, level_masks, sub3)
        for (hd, r), o_c in zip(flat, outs):
            oh_s[hd, r, :] = o_c
        for hd, st in zip(heads, states):
            st_ref[hd] = st
        return carry

    lax.fori_loop(0, nh // HG_GROUP, group_body, 0)

    og = og_ref[...]
    outs = []
    for head in range(nh):
        oh = oh_s[head]
        ms = jnp.mean(oh * oh, axis=-1, keepdims=True)
        outs.append(oh * lax.rsqrt(ms + EPS) * og[:, HG_DK * head:HG_DK * (head + 1)])
    on = jnp.concatenate(outs, axis=1) * _silu(z_s[...])
    y = jnp.dot(on.astype(BF16), wout_ref[...], preferred_element_type=F32)
    o_ref[0] = x + gate * y


def _const_spec(shape):
    return pl.BlockSpec(shape, lambda b, i: (0,) * len(shape), pipeline_mode=pl.Buffered(1))


def _hgrn_layer(x, mod, norm_g, w_in, lb, o_g, w_out):
    bsz, seq, d = x.shape
    tm = SEQ_TILE
    nh = d // HG_DK
    slab = pltpu.VMEM((nh, tm, HG_DK), F32)
    return pl.pallas_call(
        _hgrn_kernel,
        grid=(bsz, seq // tm),
        in_specs=[
            pl.BlockSpec((1, tm, d), lambda b, i: (b, i, 0)),
            pl.BlockSpec((1, 3, d), lambda b, i: (b, 0, 0)),
            _const_spec((1, d)),
            _const_spec((d, 4 * d)),
            _const_spec((1, d)),
            _const_spec((1, d)),
            _const_spec((d, d)),
        ],
        out_specs=pl.BlockSpec((1, tm, d), lambda b, i: (b, i, 0)),
        out_shape=jax.ShapeDtypeStruct(x.shape, F32),
        scratch_shapes=[
            pltpu.VMEM((nh, HG_DK, HG_DK), F32),
            slab, slab, slab, slab,
            pltpu.VMEM((tm, d), F32),
            slab,
        ],
        compiler_params=pltpu.CompilerParams(
            dimension_semantics=("arbitrary", "arbitrary"), vmem_limit_bytes=VMEM_LIMIT_BYTES),
        name="hgrn_layer",
    )(x, mod, norm_g.reshape(1, d), w_in.astype(BF16), lb.reshape(1, d), o_g.reshape(1, d),
      w_out.astype(BF16))


def _fox_kernel(x_ref, mod_ref, ng_ref, wqt_ref, wk_ref, wvt_ref, wzt_ref, wf_ref, bf_ref,
                qg_ref, kg_ref, seg_ref, segt_ref, tri3_ref, negsel_ref, wout_ref, o_ref,
                k_s, cp_s, vt_s, carry_s, h_s, wq_s, m_s, l_s, acc_s):
    b = pl.program_id(0)
    i = pl.program_id(1)
    tm = x_ref.shape[1]
    d = x_ref.shape[2]
    nh = d // FX_HD
    npair = nh // 2
    tk = tm
    row0 = pl.multiple_of(i * tm, tm)

    @pl.when((b == 0) & (i == 0))
    def _():
        wq_s[:, LANES:, :] = negsel_ref[...]

    x = x_ref[0]
    shift, scale, gate = mod_ref[0, 0:1, :], mod_ref[0, 1:2, :], mod_ref[0, 2:3, :]
    h = _modulated_norm(x, ng_ref[...], scale, shift).astype(BF16)
    h_s[...] = h

    qt3 = lax.dot_general(wqt_ref[...], h, _NT, preferred_element_type=F32).reshape(nh, FX_HD, tm)
    qms = jnp.mean(qt3 * qt3, axis=1, keepdims=True)
    qgain = (qg_ref[...] * (FX_HD ** -0.5 * LOG2E)).reshape(nh, FX_HD, 1)
    qtn = (qt3 * lax.rsqrt(qms + EPS) * qgain).astype(BF16)
    zeros = jnp.zeros((FX_HD, tm), BF16)
    for hd in range(nh):
        halves = [qtn[hd], zeros] if hd % 2 == 0 else [zeros, qtn[hd]]
        wq_s[hd, :LANES, :] = jnp.concatenate(halves, axis=0)

    k = jnp.dot(h, wk_ref[...], preferred_element_type=F32)
    ss = jnp.dot((k * k).astype(BF16), seg_ref[...], preferred_element_type=F32)
    r = lax.rsqrt(ss * (1.0 / FX_HD) + EPS)
    r_hi = r.astype(BF16)
    r_lo = (r - r_hi.astype(F32)).astype(BF16)
    rr = (jnp.dot(r_hi, segt_ref[...], preferred_element_type=F32)
          + jnp.dot(r_lo, segt_ref[...], preferred_element_type=F32))
    kn = (k * rr * kg_ref[...]).astype(BF16)
    for p in range(npair):
        k_s[p, pl.ds(row0, tm), :] = kn[:, LANES * p:LANES * (p + 1)]

    vt_s[:, pl.ds(row0, tm)] = lax.dot_general(
        wvt_ref[...], h, _NT, preferred_element_type=F32).astype(BF16)

    lane = lax.broadcasted_iota(jnp.int32, (tm, LANES), 1)
    fl = jnp.dot(h, wf_ref[...], preferred_element_type=F32)
    lf = jnp.where(lane < nh, _log_sigmoid(fl + bf_ref[...]), 0.0)
    hi, mid, lo = _split3(lf)
    cum = jnp.dot(tri3_ref[...], jnp.concatenate([hi, mid, lo], axis=0),
                  preferred_element_type=F32)
    cum = cum + jnp.where(i == 0, 0.0, carry_s[0:1, :])
    carry_s[...] = jnp.broadcast_to(cum[tm - 1:tm, :], carry_s.shape)
    chi, cmid, clo = _split3(cum * LOG2E)
    cp = (chi.astype(F32) + pltpu.roll(cmid.astype(F32), nh, axis=1)
          + pltpu.roll(clo.astype(F32), 2 * nh, axis=1))
    cp_s[pl.ds(row0, tm), :] = cp.astype(BF16)

    m_s[...] = jnp.full(m_s.shape, -jnp.inf, F32)
    l_s[...] = jnp.zeros_like(l_s)
    acc_s[...] = jnp.zeros_like(acc_s)
    key_i = lax.broadcasted_iota(jnp.int32, (tk, tm), 0)
    qry_i = lax.broadcasted_iota(jnp.int32, (tk, tm), 1)
    causal = key_i <= qry_i

    def score_fn(j):
        rows = pl.ds(pl.multiple_of(j * tk, tk), tk)
        lhs = {}

        def score(hd):
            p = hd // 2
            if p not in lhs:
                lhs[p] = jnp.concatenate([k_s[p, rows, :], cp_s[rows, :]], axis=1)
            return jnp.dot(lhs[p], wq_s[hd], preferred_element_type=F32)
        return score

    def tile_update(j, ahead, last):
        rows = pl.ds(pl.multiple_of(j * tk, tk), tk)
        score, score_next = score_fn(j), score_fn(j + 1)
        scores = dict(enumerate(ahead))
        ahead_next = []
        for hd in range(nh):
            nxt = hd + FX_LOOKAHEAD
            if nxt < nh:
                scores[nxt] = score(nxt)
            elif not last:
                ahead_next.append(score_next(nxt - nh))
            st = scores.pop(hd)
            if last:
                st = jnp.where(causal, st, -jnp.inf)
            st = st.reshape(tk // 8, 8, tm)
            m_old = m_s[hd]
            mx = jnp.max(st, axis=0)
            mx = jnp.broadcast_to(jnp.max(mx, axis=0, keepdims=True), (8, tm))
            m_new = jnp.maximum(m_old, mx)
            alpha = jnp.exp2(m_old - m_new)
            pr = jnp.exp2(st - m_new)
            m_s[hd] = m_new
            l_s[hd] = alpha * l_s[hd] + jnp.sum(pr, axis=0)
            hs = slice(FX_HD * hd, FX_HD * (hd + 1))
            pv = jnp.dot(vt_s[hs, rows], pr.reshape(tk, tm).astype(BF16),
                         preferred_element_type=F32)
            acc_s[hs, :] = jnp.tile(alpha, (FX_HD // 8, 1)) * acc_s[hs, :] + pv
        return tuple(ahead_next)

    first = score_fn(0)
    ahead = lax.fori_loop(0, i, lambda j, ahead: tile_update(j, ahead, False),
                          tuple(first(hd) for hd in range(FX_LOOKAHEAD)))
    tile_update(i, ahead, True)

    zt = lax.dot_general(wzt_ref[...], h_s[...], _NT, preferred_element_type=F32)
    inv_l = 1.0 / jnp.sum(l_s[...], axis=1, keepdims=True)
    ot = jnp.concatenate(
        [acc_s[FX_HD * hd:FX_HD * (hd + 1), :] * inv_l[hd] for hd in range(nh)], axis=0)
    on = (ot * _silu(zt)).astype(BF16)
    y = lax.dot_general(on, wout_ref[...], _TN, preferred_element_type=F32)
    o_ref[0] = x + gate * y


def _fox_layer(x, mod, norm_g, w_in, b_f, q_g, k_g, w_out):
    bsz, seq, d = x.shape
    tm = SEQ_TILE
    nh = d // FX_HD
    npair = nh // 2
    wq, wk, wv, wz, wf = (w_in[:, :d], w_in[:, d:2 * d], w_in[:, 2 * d:3 * d],
                          w_in[:, 3 * d:4 * d], w_in[:, 4 * d:])
    head_of = jnp.arange(d) // FX_HD
    seg = (head_of[:, None] == jnp.arange(LANES)[None, :]).astype(BF16)
    tri = (jnp.arange(tm)[None, :] <= jnp.arange(tm)[:, None]).astype(BF16)
    tri3 = jnp.concatenate([tri, tri, tri], axis=1)
    sel_row = jnp.arange(LANES)[None, :, None]
    sel_head = jnp.arange(nh)[:, None, None]
    negsel = -((sel_row % nh == sel_head) & (sel_row < 3 * nh)).astype(BF16)
    negsel = jnp.broadcast_to(negsel, (nh, LANES, tm))
    wf_pad = jnp.pad(wf, ((0, 0), (0, LANES - nh))).astype(BF16)
    bf_pad = jnp.pad(b_f, (0, LANES - nh)).reshape(1, LANES)
    return pl.pallas_call(
        _fox_kernel,
        grid=(bsz, seq // tm),
        in_specs=[
            pl.BlockSpec((1, tm, d), lambda b, i: (b, i, 0)),
            pl.BlockSpec((1, 3, d), lambda b, i: (b, 0, 0)),
            _const_spec((1, d)),
            _const_spec((d, d)),
            _const_spec((d, d)),
            _const_spec((d, d)),
            _const_spec((d, d)),
            _const_spec((d, LANES)),
            _const_spec((1, LANES)),
            _const_spec((d, 1)),
            _const_spec((1, d)),
            _const_spec((d, LANES)),
            _const_spec((LANES, d)),
            _const_spec((tm, 3 * tm)),
            _const_spec((nh, LANES, tm)),
            _const_spec((d, d)),
        ],
        out_specs=pl.BlockSpec((1, tm, d), lambda b, i: (b, i, 0)),
        out_shape=jax.ShapeDtypeStruct(x.shape, F32),
        scratch_shapes=[
            pltpu.VMEM((npair, seq, LANES), BF16),
            pltpu.VMEM((seq, LANES), BF16),
            pltpu.VMEM((d, seq), BF16),
            pltpu.VMEM((8, LANES), F32),
            pltpu.VMEM((tm, d), BF16),
            pltpu.VMEM((nh, 2 * LANES, tm), BF16),
            pltpu.VMEM((nh, 8, tm), F32),
            pltpu.VMEM((nh, 8, tm), F32),
            pltpu.VMEM((d, tm), F32),
        ],
        compiler_params=pltpu.CompilerParams(
            dimension_semantics=("arbitrary", "arbitrary"), vmem_limit_bytes=VMEM_LIMIT_BYTES),
        name="fox_layer",
    )(x, mod, norm_g.reshape(1, d), wq.T.astype(BF16), wk.astype(BF16), wv.T.astype(BF16),
      wz.T.astype(BF16), wf_pad, bf_pad, jnp.tile(q_g, nh).reshape(d, 1),
      jnp.tile(k_g, nh).reshape(1, d), seg, seg.T, tri3, negsel, w_out.astype(BF16))


def kernel(x, c, norm_g, ada_w, ada_b, hg_lb_logits, hg_w_in, hg_o_g, hg_w_out,
           fx_w_in, fx_b_f, fx_q_g, fx_k_g, fx_w_out):
    depth = norm_g.shape[0]
    bsz, _, d = x.shape
    lb_all = jnp.cumsum(jax.nn.softmax(hg_lb_logits.astype(F32), axis=0), axis=0)
    lb_all = lb_all - lb_all[0:1]
    mod = _adaln(c, ada_w, ada_b).reshape(depth, bsz, 3, d)
    for layer in range(depth):
        j = layer // 2
        if layer % 2 == 0:
            x = _hgrn_layer(x, mod[layer], norm_g[layer], hg_w_in[j], lb_all[j], hg_o_g[j],
                            hg_w_out[j])
        else:
            x = _fox_layer(x, mod[layer], norm_g[layer], fx_w_in[j], fx_b_f[j], fx_q_g[j],
                           fx_k_g[j], fx_w_out[j])
    return x
```

```python
import functools

import jax
import jax.numpy as jnp
from jax import lax
from jax.experimental import pallas as pl
from jax.experimental.pallas import tpu as pltpu

F32 = jnp.float32
BF16 = jnp.bfloat16

EPS = 1e-6
LOG2E = 1.4426950408889634
LANES = 128
HG_DK = 128
HG_CHUNK = 64
HG_LEVELS = (32, 16, 8, 4, 2, 1, 0)
HG_GROUP = 8
FX_HD = 64
FX_LOOKAHEAD = 4
FX_FIXED_REF_SPAN = 96.0
HG_SEQ_TILE = 256
FX_SEQ_TILE = 256
VMEM_LIMIT_BYTES = 56 * 1024 * 1024

_NT = (((1,), (1,)), ((), ()))
_TN = (((0,), (0,)), ((), ()))


def _split3(x):
    hi = x.astype(BF16)
    r1 = x - hi.astype(F32)
    mid = r1.astype(BF16)
    lo = (r1 - mid.astype(F32)).astype(BF16)
    return hi, mid, lo


def _modulated_norm(x, g, scale, shift):
    ms = jnp.mean(x * x, axis=-1, keepdims=True)
    return (x * lax.rsqrt(ms + EPS) * g) * (1.0 + scale) + shift


def _silu(x):
    return x / (1.0 + jnp.exp(-x))


def _log_sigmoid(x):
    return jnp.minimum(x, 0.0) - jnp.log1p(jnp.exp(-jnp.abs(x)))


def _adaln_kernel(c_ref, w_ref, b_ref, o_ref):
    c_act = _silu(c_ref[...])
    o_ref[0] = jnp.dot(c_act, w_ref[0], precision=lax.Precision.HIGHEST,
                       preferred_element_type=F32) + b_ref[0]


def _adaln(c, ada_w, ada_b):
    depth, d, n = ada_w.shape
    b = c.shape[0]
    tn = 1024
    return pl.pallas_call(
        _adaln_kernel,
        grid=(depth, n // tn),
        in_specs=[
            pl.BlockSpec((b, d), lambda l, j: (0, 0)),
            pl.BlockSpec((1, d, tn), lambda l, j: (l, 0, j)),
            pl.BlockSpec((1, 1, tn), lambda l, j: (l, 0, j)),
        ],
        out_specs=pl.BlockSpec((1, b, tn), lambda l, j: (l, 0, j)),
        out_shape=jax.ShapeDtypeStruct((depth, b, n), F32),
        compiler_params=pltpu.CompilerParams(
            dimension_semantics=("arbitrary", "arbitrary"), vmem_limit_bytes=VMEM_LIMIT_BYTES),
        name="adaln_mod",
    )(c, ada_w, ada_b.reshape(depth, 1, n))


def _hgrn_level_operands(q, k, b, lf, half, sub3):
    c = HG_CHUNK
    if half == 0:
        return q, k
    if half >= 8:
        qp, kp = [], []
        for g in range(c // (2 * half)):
            lo_r = slice(2 * half * g, 2 * half * g + half)
            up_r = slice(2 * half * g + half, 2 * half * (g + 1))
            ref = b[2 * half * g + half - 1:2 * half * g + half, :]
            zeros = jnp.zeros((half, HG_DK), F32)
            qp += [zeros, q[up_r] * jnp.exp2(b[up_r] - ref)]
            kp += [k[lo_r] * jnp.exp2(ref - b[lo_r]), zeros]
        return jnp.concatenate(qp, axis=0), jnp.concatenate(kp, axis=0)
    shape3 = (c // 8, 8, HG_DK)
    q3, k3, b3 = q.reshape(shape3), k.reshape(shape3), b.reshape(shape3)
    if half == 1:
        upper = sub3 % 2 == 1
        qp = jnp.where(upper, q3 * jnp.exp2(lf.reshape(shape3)), 0.0)
        kp = jnp.where(upper, 0.0, k3)
    else:
        if half == 4:
            ref = b3[:, 3:4, :]
        else:
            ref = jnp.where(sub3 < 4, b3[:, 1:2, :], b3[:, 5:6, :])
        upper = sub3 % (2 * half) >= half
        w = jnp.exp2(-jnp.abs(b3 - ref))
        qp = jnp.where(upper, q3 * w, 0.0)
        kp = jnp.where(upper, 0.0, k3 * w)
    return qp.reshape(c, HG_DK), kp.reshape(c, HG_DK)


def _hgrn_heads(qs, lfs, ks, vs, states, tri3, level_masks, sub3):
    c = HG_CHUNK
    n = len(qs)
    per_head = n // len(states)
    bs = [jnp.dot(tri3, jnp.concatenate(_split3(lf), axis=0), preferred_element_type=F32)
          for lf in lfs]
    vbs = [v.astype(BF16) for v in vs]
    a_sums = []
    for i in range(n):
        a_sum = None
        for half, mask in zip(HG_LEVELS, level_masks):
            qp, kp = _hgrn_level_operands(qs[i], ks[i], bs[i], lfs[i], half, sub3)
            a_l = lax.dot_general(qp.astype(BF16), kp.astype(BF16), _NT,
                                  preferred_element_type=F32)
            a_l = a_l if mask is None else jnp.where(mask, a_l, 0.0)
            a_sum = a_l if a_sum is None else a_sum + a_l
        a_sums.append(a_sum)
    o_intra = [jnp.dot(a_sums[i].astype(BF16), vbs[i], preferred_element_type=F32)
               for i in range(n)]
    blasts = [b[c - 1:c, :] for b in bs]
    q1s = [(qs[i] * jnp.exp2(bs[i])).astype(BF16) for i in range(n)]
    kds = [(ks[i] * jnp.exp2(blasts[i] - bs[i])).astype(BF16) for i in range(n)]
    pcs = [lax.dot_general(vbs[i], kds[i], _TN, preferred_element_type=F32) for i in range(n)]
    st_in, st_out = [], []
    for h, st in enumerate(states):
        for i in range(per_head * h, per_head * (h + 1)):
            st_in.append(st)
            st = st * jnp.exp2(blasts[i]) + pcs[i]
        st_out.append(st)
    outs = [o_intra[i] + lax.dot_general(q1s[i], st_in[i].astype(BF16), _NT,
                                         preferred_element_type=F32) for i in range(n)]
    return outs, st_out


def _hgrn_kernel(x_ref, mod_ref, ng_ref, win_ref, lb_ref, og_ref, wout_ref, o_ref,
                 st_ref, q_s, lf_s, k_s, v_s, z_s, oh_s):
    i = pl.program_id(1)
    tm = x_ref.shape[1]
    d = x_ref.shape[2]
    nh = d // HG_DK
    c = HG_CHUNK

    @pl.when(i == 0)
    def _():
        st_ref[...] = jnp.zeros_like(st_ref)

    x = x_ref[0]
    shift, scale, gate = mod_ref[0, 0:1, :], mod_ref[0, 1:2, :], mod_ref[0, 2:3, :]
    h = _modulated_norm(x, ng_ref[...], scale, shift).astype(BF16)

    lb = lb_ref[...]
    l0, l1, oml = jnp.log(lb), jnp.log1p(-lb), 1.0 - lb
    for blk in range(4 * d // 256):
        p = jnp.dot(h, win_ref[:, 256 * blk:256 * (blk + 1)], preferred_element_type=F32)
        sect, hp = divmod(blk, nh // 2)
        for hh in range(2):
            head = 2 * hp + hh
            ph = p[:, HG_DK * hh:HG_DK * (hh + 1)]
            if sect == 0:
                q_s[head] = ph
            elif sect == 1:
                lbh = slice(HG_DK * head, HG_DK * (head + 1))
                e = jnp.exp(-jnp.abs(ph))
                lp = jnp.log1p(e)
                cc = l1[:, lbh] + (jnp.minimum(ph, 0.0) - lp)
                lf_s[head] = jnp.logaddexp(l0[:, lbh], cc) * LOG2E
                k_s[head] = oml[:, lbh] * (jnp.where(ph >= 0.0, e, 1.0) / (1.0 + e))
            elif sect == 2:
                v_s[head] = ph
            else:
                z_s[:, HG_DK * head:HG_DK * (head + 1)] = ph

    ri = lax.broadcasted_iota(jnp.int32, (c, 3 * c), 0)
    ci = lax.broadcasted_iota(jnp.int32, (c, 3 * c), 1)
    tri3 = ((ci % c) <= ri).astype(BF16)
    ti = lax.broadcasted_iota(jnp.int32, (c, c), 0)
    si = lax.broadcasted_iota(jnp.int32, (c, c), 1)
    level_masks = [None if 2 * half == c else
                   (ti == si if half == 0 else (ti // (2 * half)) == (si // (2 * half)))
                   for half in HG_LEVELS]
    sub3 = lax.broadcasted_iota(jnp.int32, (1, 8, HG_DK), 1)
    chunks = [slice(c * ch, c * (ch + 1)) for ch in range(tm // c)]

    def group_body(g, carry):
        heads = [g * HG_GROUP + hh for hh in range(HG_GROUP)]
        flat = [(hd, r) for hd in heads for r in chunks]
        outs, states = _hgrn_heads([q_s[hd, r, :] for hd, r in flat],
                                   [lf_s[hd, r, :] for hd, r in flat],
                                   [k_s[hd, r, :] for hd, r in flat],
                                   [v_s[hd, r, :] for hd, r in flat],
                                   [st_ref[hd] for hd in heads], tri3, level_masks, sub3)
        for (hd, r), o_c in zip(flat, outs):
            oh_s[hd, r, :] = o_c
        for hd, st in zip(heads, states):
            st_ref[hd] = st
        return carry

    lax.fori_loop(0, nh // HG_GROUP, group_body, 0)

    og = og_ref[...]
    outs = []
    for head in range(nh):
        oh = oh_s[head]
        ms = jnp.mean(oh * oh, axis=-1, keepdims=True)
        outs.append(oh * lax.rsqrt(ms + EPS) * og[:, HG_DK * head:HG_DK * (head + 1)])
    on = jnp.concatenate(outs, axis=1) * _silu(z_s[...])
    y = jnp.dot(on.astype(BF16), wout_ref[...], preferred_element_type=F32)
    o_ref[0] = x + gate * y


def _const_spec(shape):
    return pl.BlockSpec(shape, lambda b, i: (0,) * len(shape), pipeline_mode=pl.Buffered(1))


def _hgrn_layer(x, mod, norm_g, w_in, lb, o_g, w_out):
    bsz, seq, d = x.shape
    tm = HG_SEQ_TILE
    nh = d // HG_DK
    slab = pltpu.VMEM((nh, tm, HG_DK), F32)
    return pl.pallas_call(
        _hgrn_kernel,
        grid=(bsz, seq // tm),
        in_specs=[
            pl.BlockSpec((1, tm, d), lambda b, i: (b, i, 0)),
            pl.BlockSpec((1, 3, d), lambda b, i: (b, 0, 0)),
            _const_spec((1, d)),
            _const_spec((d, 4 * d)),
            _const_spec((1, d)),
            _const_spec((1, d)),
            _const_spec((d, d)),
        ],
        out_specs=pl.BlockSpec((1, tm, d), lambda b, i: (b, i, 0)),
        out_shape=jax.ShapeDtypeStruct(x.shape, F32),
        scratch_shapes=[
            pltpu.VMEM((nh, HG_DK, HG_DK), F32),
            slab, slab, slab, slab,
            pltpu.VMEM((tm, d), F32),
            slab,
        ],
        compiler_params=pltpu.CompilerParams(
            dimension_semantics=("arbitrary", "arbitrary"), vmem_limit_bytes=VMEM_LIMIT_BYTES),
        name="hgrn_layer",
    )(x, mod, norm_g.reshape(1, d), w_in.astype(BF16), lb.reshape(1, d), o_g.reshape(1, d),
      w_out.astype(BF16))


def _fox_kernel(fixed_ref, x_ref, mod_ref, ng_ref, wqt_ref, wk_ref, wvt_ref, wzt_ref, wf_ref,
                bf_ref, qg_ref, kg_ref, seg_ref, segt_ref, tri3_ref, negsel_ref, bound_ref, wout_ref,
                o_ref, k_s, cp_s, vt_s, carry_s, h_s, wq_s, m_s, l_s, acc_s):
    b = pl.program_id(0)
    i = pl.program_id(1)
    tm = x_ref.shape[1]
    d = x_ref.shape[2]
    nh = d // FX_HD
    npair = nh // 2
    tk = tm
    row0 = pl.multiple_of(i * tm, tm)

    @pl.when((b == 0) & (i == 0))
    def _():
        wq_s[:, LANES:, :] = negsel_ref[...]

    x = x_ref[0]
    shift, scale, gate = mod_ref[0, 0:1, :], mod_ref[0, 1:2, :], mod_ref[0, 2:3, :]
    h = _modulated_norm(x, ng_ref[...], scale, shift).astype(BF16)
    h_s[...] = h

    qt3 = lax.dot_general(wqt_ref[...], h, _NT, preferred_element_type=F32).reshape(nh, FX_HD, tm)
    qms = jnp.mean(qt3 * qt3, axis=1, keepdims=True)
    qgain = (qg_ref[...] * (FX_HD ** -0.5 * LOG2E)).reshape(nh, FX_HD, 1)
    qtn = (qt3 * lax.rsqrt(qms + EPS) * qgain).astype(BF16)
    zeros = jnp.zeros((FX_HD, tm), BF16)
    for hd in range(nh):
        halves = [qtn[hd], zeros] if hd % 2 == 0 else [zeros, qtn[hd]]
        wq_s[hd, :LANES, :] = jnp.concatenate(halves, axis=0)

    k = jnp.dot(h, wk_ref[...], preferred_element_type=F32)
    ss = jnp.dot((k * k).astype(BF16), seg_ref[...], preferred_element_type=F32)
    r = lax.rsqrt(ss * (1.0 / FX_HD) + EPS)
    r_hi = r.astype(BF16)
    r_lo = (r - r_hi.astype(F32)).astype(BF16)
    rr = (jnp.dot(r_hi, segt_ref[...], preferred_element_type=F32)
          + jnp.dot(r_lo, segt_ref[...], preferred_element_type=F32))
    kn = (k * rr * kg_ref[...]).astype(BF16)
    for p in range(npair):
        k_s[p, pl.ds(row0, tm), :] = kn[:, LANES * p:LANES * (p + 1)]

    vt_s[:, pl.ds(row0, tm)] = lax.dot_general(
        wvt_ref[...], h, _NT, preferred_element_type=F32).astype(BF16)

    lane = lax.broadcasted_iota(jnp.int32, (tm, LANES), 1)
    fl = jnp.dot(h, wf_ref[...], preferred_element_type=F32)
    lf = jnp.where(lane < nh, _log_sigmoid(fl + bf_ref[...]), 0.0)
    hi, mid, lo = _split3(lf)
    cum = jnp.dot(tri3_ref[...], jnp.concatenate([hi, mid, lo], axis=0),
                  preferred_element_type=F32)
    cum = cum + jnp.where(i == 0, 0.0, carry_s[0:1, :])
    carry_s[...] = jnp.broadcast_to(cum[tm - 1:tm, :], carry_s.shape)
    chi, cmid, clo = _split3(cum * LOG2E)
    cp = (chi.astype(F32) + pltpu.roll(cmid.astype(F32), nh, axis=1)
          + pltpu.roll(clo.astype(F32), 2 * nh, axis=1))
    if fixed_ref:
        cp = cp + jnp.where((lane >= 3 * nh) & (lane < 3 * nh + 3), 1.0, 0.0)
        ref = jnp.transpose(cum * LOG2E)[:nh, :] - bound_ref[...]
        sub16 = lax.broadcasted_iota(jnp.int32, (16, tm), 0)
        for hd, pieces in enumerate(zip(*[p.astype(F32) for p in _split3(ref)])):
            rhi, rmid, rlo = [p.reshape(1, tm) for p in pieces]
            blk = jnp.where(sub16 == 0, rhi, jnp.where(sub16 == 1, rmid,
                                                       jnp.where(sub16 == 2, rlo, 0.0)))
            wq_s[hd, LANES + 3 * nh:LANES + 3 * nh + 16, :] = blk.astype(BF16)
    else:
        m_s[...] = jnp.full(m_s.shape, -jnp.inf, F32)
    cp_s[pl.ds(row0, tm), :] = cp.astype(BF16)

    l_s[...] = jnp.zeros_like(l_s)
    acc_s[...] = jnp.zeros_like(acc_s)
    key_i = lax.broadcasted_iota(jnp.int32, (tk, tm), 0)
    qry_i = lax.broadcasted_iota(jnp.int32, (tk, tm), 1)
    causal = key_i <= qry_i

    def score_fn(j):
        rows = pl.ds(pl.multiple_of(j * tk, tk), tk)
        lhs = {}

        def score(hd):
            p = hd // 2
            if p not in lhs:
                lhs[p] = jnp.concatenate([k_s[p, rows, :], cp_s[rows, :]], axis=1)
            return jnp.dot(lhs[p], wq_s[hd], preferred_element_type=F32)
        return score

    def tile_update(j, ahead, last):
        rows = pl.ds(pl.multiple_of(j * tk, tk), tk)
        score, score_next = score_fn(j), score_fn(j + 1)
        scores = dict(enumerate(ahead))
        ahead_next = []
        for hd in range(nh):
            nxt = hd + FX_LOOKAHEAD
            if nxt < nh:
                scores[nxt] = score(nxt)
            elif not last:
                ahead_next.append(score_next(nxt - nh))
            st = scores.pop(hd)
            if last:
                st = jnp.where(causal, st, -jnp.inf)
            st = st.reshape(tk // 8, 8, tm)
            hs = slice(FX_HD * hd, FX_HD * (hd + 1))
            if fixed_ref:
                pr = jnp.exp2(st)
                l_s[hd] = l_s[hd] + jnp.sum(pr, axis=0)
                acc_s[hs, :] = acc_s[hs, :] + jnp.dot(
                    vt_s[hs, rows], pr.reshape(tk, tm).astype(BF16), preferred_element_type=F32)
                continue
            m_old = m_s[hd]
            mx = jnp.max(st, axis=0)
            mx = jnp.broadcast_to(jnp.max(mx, axis=0, keepdims=True), (8, tm))
            m_new = jnp.maximum(m_old, mx)
            alpha = jnp.exp2(m_old - m_new)
            pr = jnp.exp2(st - m_new)
            m_s[hd] = m_new
            l_s[hd] = alpha * l_s[hd] + jnp.sum(pr, axis=0)
            pv = jnp.dot(vt_s[hs, rows], pr.reshape(tk, tm).astype(BF16),
                         preferred_element_type=F32)
            acc_s[hs, :] = jnp.tile(alpha, (FX_HD // 8, 1)) * acc_s[hs, :] + pv
        return tuple(ahead_next)

    first = score_fn(0)
    ahead = lax.fori_loop(0, i, lambda j, ahead: tile_update(j, ahead, False),
                          tuple(first(hd) for hd in range(FX_LOOKAHEAD)))
    tile_update(i, ahead, True)

    zt = lax.dot_general(wzt_ref[...], h_s[...], _NT, preferred_element_type=F32)
    inv_l = 1.0 / jnp.sum(l_s[...], axis=1, keepdims=True)
    ot = jnp.concatenate(
        [acc_s[FX_HD * hd:FX_HD * (hd + 1), :] * inv_l[hd] for hd in range(nh)], axis=0)
    on = (ot * _silu(zt)).astype(BF16)
    y = lax.dot_general(on, wout_ref[...], _TN, preferred_element_type=F32)
    o_ref[0] = x + gate * y


def _fox_layer(x, mod, norm_g, w_in, b_f, q_g, k_g, w_out):
    bsz, seq, d = x.shape
    tm = FX_SEQ_TILE
    nh = d // FX_HD
    npair = nh // 2
    wq, wk, wv, wz, wf = (w_in[:, :d], w_in[:, d:2 * d], w_in[:, 2 * d:3 * d],
                          w_in[:, 3 * d:4 * d], w_in[:, 4 * d:])
    head_of = jnp.arange(d) // FX_HD
    seg = (head_of[:, None] == jnp.arange(LANES)[None, :]).astype(BF16)
    tri = (jnp.arange(tm)[None, :] <= jnp.arange(tm)[:, None]).astype(BF16)
    tri3 = jnp.concatenate([tri, tri, tri], axis=1)
    sel_row = jnp.arange(LANES)[None, :, None]
    sel_head = jnp.arange(nh)[:, None, None]
    negsel = -((sel_row % nh == sel_head) & (sel_row < 3 * nh)).astype(BF16)
    negsel = jnp.broadcast_to(negsel, (nh, LANES, tm))
    wf_pad = jnp.pad(wf, ((0, 0), (0, LANES - nh))).astype(BF16)
    bf_pad = jnp.pad(b_f, (0, LANES - nh)).reshape(1, LANES)
    bound = 1.01 * FX_HD ** 0.5 * LOG2E * jnp.max(jnp.abs(q_g)) * jnp.max(jnp.abs(k_g))
    operands = (x, mod, norm_g.reshape(1, d), wq.T.astype(BF16), wk.astype(BF16), wv.T.astype(BF16),
                wz.T.astype(BF16), wf_pad, bf_pad, jnp.tile(q_g, nh).reshape(d, 1),
                jnp.tile(k_g, nh).reshape(1, d), seg, seg.T, tri3, negsel, bound.reshape(1, 1),
                w_out.astype(BF16))
    return lax.cond(2.0 * bound <= FX_FIXED_REF_SPAN,
                    functools.partial(_fox_call, True), functools.partial(_fox_call, False),
                    *operands)


def _fox_call(fixed_ref, *operands):
    bsz, seq, d = operands[0].shape
    tm = FX_SEQ_TILE
    nh = d // FX_HD
    npair = nh // 2
    return pl.pallas_call(
        functools.partial(_fox_kernel, fixed_ref),
        grid=(bsz, seq // tm),
        in_specs=[
            pl.BlockSpec((1, tm, d), lambda b, i: (b, i, 0)),
            pl.BlockSpec((1, 3, d), lambda b, i: (b, 0, 0)),
            _const_spec((1, d)),
            _const_spec((d, d)),
            _const_spec((d, d)),
            _const_spec((d, d)),
            _const_spec((d, d)),
            _const_spec((d, LANES)),
            _const_spec((1, LANES)),
            _const_spec((d, 1)),
            _const_spec((1, d)),
            _const_spec((d, LANES)),
            _const_spec((LANES, d)),
            _const_spec((tm, 3 * tm)),
            _const_spec((nh, LANES, tm)),
            _const_spec((1, 1)),
            _const_spec((d, d)),
        ],
        out_specs=pl.BlockSpec((1, tm, d), lambda b, i: (b, i, 0)),
        out_shape=jax.ShapeDtypeStruct(operands[0].shape, F32),
        scratch_shapes=[
            pltpu.VMEM((npair, seq, LANES), BF16),
            pltpu.VMEM((seq, LANES), BF16),
            pltpu.VMEM((d, seq), BF16),
            pltpu.VMEM((8, LANES), F32),
            pltpu.VMEM((tm, d), BF16),
            pltpu.VMEM((nh, 2 * LANES, tm), BF16),
            pltpu.VMEM((nh, 8, tm), F32),
            pltpu.VMEM((nh, 8, tm), F32),
            pltpu.VMEM((d, tm), F32),
        ],
        compiler_params=pltpu.CompilerParams(
            dimension_semantics=("arbitrary", "arbitrary"), vmem_limit_bytes=VMEM_LIMIT_BYTES),
        name="fox_layer_fixed_ref" if fixed_ref else "fox_layer_running_max",
    )(*operands)


def kernel(x, c, norm_g, ada_w, ada_b, hg_lb_logits, hg_w_in, hg_o_g, hg_w_out,
           fx_w_in, fx_b_f, fx_q_g, fx_k_g, fx_w_out):
    depth = norm_g.shape[0]
    bsz, _, d = x.shape
    lb_all = jnp.cumsum(jax.nn.softmax(hg_lb_logits.astype(F32), axis=0), axis=0)
    lb_all = lb_all - lb_all[0:1]
    mod = _adaln(c, ada_w, ada_b).reshape(depth, bsz, 3, d)
    for layer in range(depth):
        j = layer // 2
        if layer % 2 == 0:
            x = _hgrn_layer(x, mod[layer], norm_g[layer], hg_w_in[j], lb_all[j], hg_o_g[j],
                            hg_w_out[j])
        else:
            x = _fox_layer(x, mod[layer], norm_g[layer], fx_w_in[j], fx_b_f[j], fx_q_g[j],
                           fx_k_g[j], fx_w_out[j])
    return x
```

```python
import functools

import jax
import jax.numpy as jnp
from jax import lax
from jax.experimental import pallas as pl
from jax.experimental.pallas import tpu as pltpu

F32 = jnp.float32
BF16 = jnp.bfloat16

EPS = 1e-6
LOG2E = 1.4426950408889634
LANES = 128
HG_DK = 128
HG_CHUNK = 64
HG_LEVELS = (32, 16, 8, 4, 2, 1, 0)
HG_GROUP = 8
FX_HD = 64
FX_LOOKAHEAD = 4
FX_FIXED_REF_SPAN = 96.0
HG_SEQ_TILE = 256
FX_SEQ_TILE = 256
VMEM_LIMIT_BYTES = 56 * 1024 * 1024

_NT = (((1,), (1,)), ((), ()))
_TN = (((0,), (0,)), ((), ()))


def _split3(x):
    hi = x.astype(BF16)
    r1 = x - hi.astype(F32)
    mid = r1.astype(BF16)
    lo = (r1 - mid.astype(F32)).astype(BF16)
    return hi, mid, lo


def _modulated_norm(x, g, scale, shift):
    ms = jnp.mean(x * x, axis=-1, keepdims=True)
    return (x * lax.rsqrt(ms + EPS) * g) * (1.0 + scale) + shift


def _silu(x):
    return x / (1.0 + jnp.exp(-x))


def _log_sigmoid(x):
    return jnp.minimum(x, 0.0) - jnp.log(1.0 + jnp.exp(-jnp.abs(x)))


def _adaln_kernel(c_ref, w_ref, b_ref, o_ref):
    c_act = _silu(c_ref[...])
    o_ref[0] = jnp.dot(c_act, w_ref[0], precision=lax.Precision.HIGHEST,
                       preferred_element_type=F32) + b_ref[0]


def _adaln(c, ada_w, ada_b):
    depth, d, n = ada_w.shape
    b = c.shape[0]
    tn = 1024
    return pl.pallas_call(
        _adaln_kernel,
        grid=(depth, n // tn),
        in_specs=[
            pl.BlockSpec((b, d), lambda l, j: (0, 0)),
            pl.BlockSpec((1, d, tn), lambda l, j: (l, 0, j)),
            pl.BlockSpec((1, 1, tn), lambda l, j: (l, 0, j)),
        ],
        out_specs=pl.BlockSpec((1, b, tn), lambda l, j: (l, 0, j)),
        out_shape=jax.ShapeDtypeStruct((depth, b, n), F32),
        compiler_params=pltpu.CompilerParams(
            dimension_semantics=("arbitrary", "arbitrary"), vmem_limit_bytes=VMEM_LIMIT_BYTES),
        name="adaln_mod",
    )(c, ada_w, ada_b.reshape(depth, 1, n))


def _hgrn_level_operands(q, k, b, lf, half, sub3):
    c = HG_CHUNK
    if half == 0:
        return q, k
    if half >= 8:
        qp, kp = [], []
        for g in range(c // (2 * half)):
            lo_r = slice(2 * half * g, 2 * half * g + half)
            up_r = slice(2 * half * g + half, 2 * half * (g + 1))
            ref = b[2 * half * g + half - 1:2 * half * g + half, :]
            zeros = jnp.zeros((half, HG_DK), F32)
            qp += [zeros, q[up_r] * jnp.exp2(b[up_r] - ref)]
            kp += [k[lo_r] * jnp.exp2(ref - b[lo_r]), zeros]
        return jnp.concatenate(qp, axis=0), jnp.concatenate(kp, axis=0)
    shape3 = (c // 8, 8, HG_DK)
    q3, k3, b3 = q.reshape(shape3), k.reshape(shape3), b.reshape(shape3)
    if half == 1:
        upper = sub3 % 2 == 1
        qp = jnp.where(upper, q3 * jnp.exp2(lf.reshape(shape3)), 0.0)
        kp = jnp.where(upper, 0.0, k3)
    else:
        if half == 4:
            ref = b3[:, 3:4, :]
        else:
            ref = jnp.where(sub3 < 4, b3[:, 1:2, :], b3[:, 5:6, :])
        upper = sub3 % (2 * half) >= half
        w = jnp.exp2(-jnp.abs(b3 - ref))
        qp = jnp.where(upper, q3 * w, 0.0)
        kp = jnp.where(upper, 0.0, k3 * w)
    return qp.reshape(c, HG_DK), kp.reshape(c, HG_DK)


def _hgrn_heads(qs, lfs, ks, vs, states, tri3, level_masks, sub3):
    c = HG_CHUNK
    n = len(qs)
    per_head = n // len(states)
    bs = [jnp.dot(tri3, jnp.concatenate(_split3(lf), axis=0), preferred_element_type=F32)
          for lf in lfs]
    vbs = [v.astype(BF16) for v in vs]
    a_sums = []
    for i in range(n):
        a_sum = None
        for half, mask in zip(HG_LEVELS, level_masks):
            qp, kp = _hgrn_level_operands(qs[i], ks[i], bs[i], lfs[i], half, sub3)
            a_l = lax.dot_general(qp.astype(BF16), kp.astype(BF16), _NT,
                                  preferred_element_type=F32)
            a_l = a_l if mask is None else jnp.where(mask, a_l, 0.0)
            a_sum = a_l if a_sum is None else a_sum + a_l
        a_sums.append(a_sum)
    o_intra = [jnp.dot(a_sums[i].astype(BF16), vbs[i], preferred_element_type=F32)
               for i in range(n)]
    blasts = [b[c - 1:c, :] for b in bs]
    q1s = [(qs[i] * jnp.exp2(bs[i])).astype(BF16) for i in range(n)]
    kds = [(ks[i] * jnp.exp2(blasts[i] - bs[i])).astype(BF16) for i in range(n)]
    pcs = [lax.dot_general(vbs[i], kds[i], _TN, preferred_element_type=F32) for i in range(n)]
    st_in, st_out = [], []
    for h, st in enumerate(states):
        for i in range(per_head * h, per_head * (h + 1)):
            st_in.append(st)
            st = st * jnp.exp2(blasts[i]) + pcs[i]
        st_out.append(st)
    outs = [o_intra[i] + lax.dot_general(q1s[i], st_in[i].astype(BF16), _NT,
                                         preferred_element_type=F32) for i in range(n)]
    return outs, st_out


def _hgrn_kernel(lb_is_zero, x_ref, mod_ref, ng_ref, win_ref, lb_ref, og_ref, wout_ref, o_ref,
                 st_ref, q_s, lf_s, k_s, v_s, z_s, oh_s):
    i = pl.program_id(1)
    tm = x_ref.shape[1]
    d = x_ref.shape[2]
    nh = d // HG_DK
    c = HG_CHUNK

    @pl.when(i == 0)
    def _():
        st_ref[...] = jnp.zeros_like(st_ref)

    x = x_ref[0]
    shift, scale, gate = mod_ref[0, 0:1, :], mod_ref[0, 1:2, :], mod_ref[0, 2:3, :]
    h = _modulated_norm(x, ng_ref[...], scale, shift).astype(BF16)

    if not lb_is_zero:
        lb = lb_ref[...]
        l0, l1, oml = jnp.log(lb), jnp.log1p(-lb), 1.0 - lb
    blocks = [(sect, hp) for hp in range(nh // 2) for sect in range(3)]
    blocks += [(3, hp) for hp in range(nh // 2)]
    for sect, hp in blocks:
        blk = sect * (nh // 2) + hp
        p = jnp.dot(h, win_ref[:, 256 * blk:256 * (blk + 1)], preferred_element_type=F32)
        for hh in range(2):
            head = 2 * hp + hh
            ph = p[:, HG_DK * hh:HG_DK * (hh + 1)]
            if sect == 0:
                q_s[head] = ph
            elif sect == 1:
                lbh = slice(HG_DK * head, HG_DK * (head + 1))
                e = jnp.exp(-jnp.abs(ph))
                sig_neg = jnp.where(ph >= 0.0, e, 1.0) / (1.0 + e)
                if lb_is_zero:
                    lf_s[head] = jnp.minimum(ph, 0.0) * LOG2E - jnp.log2(1.0 + e)
                    k_s[head] = sig_neg
                else:
                    cc = l1[:, lbh] + (jnp.minimum(ph, 0.0) - jnp.log(1.0 + e))
                    hi_arg = jnp.maximum(l0[:, lbh], cc)
                    lf_s[head] = (hi_arg + jnp.log(1.0 + jnp.exp(-jnp.abs(l0[:, lbh] - cc)))) * LOG2E
                    k_s[head] = oml[:, lbh] * sig_neg
            elif sect == 2:
                v_s[head] = ph
            else:
                z_s[:, HG_DK * head:HG_DK * (head + 1)] = _silu(ph)

    ri = lax.broadcasted_iota(jnp.int32, (c, 3 * c), 0)
    ci = lax.broadcasted_iota(jnp.int32, (c, 3 * c), 1)
    tri3 = ((ci % c) <= ri).astype(BF16)
    ti = lax.broadcasted_iota(jnp.int32, (c, c), 0)
    si = lax.broadcasted_iota(jnp.int32, (c, c), 1)
    level_masks = [None if 2 * half == c else
                   (ti == si if half == 0 else (ti // (2 * half)) == (si // (2 * half)))
                   for half in HG_LEVELS]
    sub3 = lax.broadcasted_iota(jnp.int32, (1, 8, HG_DK), 1)
    chunks = [slice(c * ch, c * (ch + 1)) for ch in range(tm // c)]

    def group_body(g, carry):
        heads = [g * HG_GROUP + hh for hh in range(HG_GROUP)]
        flat = [(hd, r) for hd in heads for r in chunks]
        outs, states = _hgrn_heads([q_s[hd, r, :] for hd, r in flat],
                                   [lf_s[hd, r, :] for hd, r in flat],
                                   [k_s[hd, r, :] for hd, r in flat],
                                   [v_s[hd, r, :] for hd, r in flat],
                                   [st_ref[hd] for hd in heads], tri3, level_masks, sub3)
        for (hd, r), o_c in zip(flat, outs):
            oh_s[hd, r, :] = o_c
        for hd, st in zip(heads, states):
            st_ref[hd] = st
        return carry

    lax.fori_loop(0, nh // HG_GROUP, group_body, 0)

    og = og_ref[...]
    y = None
    for hp in range(nh // 2):
        outs = []
        for head in (2 * hp, 2 * hp + 1):
            oh = oh_s[head]
            ms = jnp.mean(oh * oh, axis=-1, keepdims=True)
            outs.append(oh * lax.rsqrt(ms + EPS) * og[:, HG_DK * head:HG_DK * (head + 1)])
        cols = slice(2 * HG_DK * hp, 2 * HG_DK * (hp + 1))
        on = jnp.concatenate(outs, axis=1) * z_s[:, cols]
        y_p = jnp.dot(on.astype(BF16), wout_ref[cols, :], preferred_element_type=F32)
        y = y_p if y is None else y + y_p
    o_ref[0] = x + gate * y


def _const_spec(shape):
    return pl.BlockSpec(shape, lambda b, i: (0,) * len(shape), pipeline_mode=pl.Buffered(1))


def _hgrn_layer(x, mod, norm_g, w_in, lb, o_g, w_out, lb_is_zero):
    bsz, seq, d = x.shape
    tm = HG_SEQ_TILE
    nh = d // HG_DK
    slab = pltpu.VMEM((nh, tm, HG_DK), F32)
    return pl.pallas_call(
        functools.partial(_hgrn_kernel, lb_is_zero),
        grid=(bsz, seq // tm),
        in_specs=[
            pl.BlockSpec((1, tm, d), lambda b, i: (b, i, 0)),
            pl.BlockSpec((1, 3, d), lambda b, i: (b, 0, 0)),
            _const_spec((1, d)),
            _const_spec((d, 4 * d)),
            _const_spec((1, d)),
            _const_spec((1, d)),
            _const_spec((d, d)),
        ],
        out_specs=pl.BlockSpec((1, tm, d), lambda b, i: (b, i, 0)),
        out_shape=jax.ShapeDtypeStruct(x.shape, F32),
        scratch_shapes=[
            pltpu.VMEM((nh, HG_DK, HG_DK), F32),
            slab, slab, slab, slab,
            pltpu.VMEM((tm, d), F32),
            slab,
        ],
        compiler_params=pltpu.CompilerParams(
            dimension_semantics=("arbitrary", "arbitrary"), vmem_limit_bytes=VMEM_LIMIT_BYTES),
        name="hgrn_layer",
    )(x, mod, norm_g.reshape(1, d), w_in.astype(BF16), lb.reshape(1, d), o_g.reshape(1, d),
      w_out.astype(BF16))


def _fox_kernel(fixed_ref, x_ref, mod_ref, ng_ref, wqt_ref, wk_ref, wvt_ref, wzt_ref, wf_ref,
                bf_ref, qg_ref, kg_ref, seg_ref, segt_ref, tri3_ref, negsel_ref, bound_ref, wout_ref,
                o_ref, k_s, cp_s, vt_s, carry_s, sz_s, wq_s, m_s, l_s, acc_s):
    b = pl.program_id(0)
    i = pl.program_id(1)
    tm = x_ref.shape[1]
    d = x_ref.shape[2]
    nh = d // FX_HD
    npair = nh // 2
    tk = tm
    row0 = pl.multiple_of(i * tm, tm)

    @pl.when((b == 0) & (i == 0))
    def _():
        wq_s[:, LANES:, :] = negsel_ref[...]

    x = x_ref[0]
    shift, scale, gate = mod_ref[0, 0:1, :], mod_ref[0, 1:2, :], mod_ref[0, 2:3, :]
    h = _modulated_norm(x, ng_ref[...], scale, shift).astype(BF16)

    qt3 = lax.dot_general(wqt_ref[...], h, _NT, preferred_element_type=F32).reshape(nh, FX_HD, tm)
    k = jnp.dot(h, wk_ref[...], preferred_element_type=F32)
    fl = jnp.dot(h, wf_ref[...], preferred_element_type=F32)
    vt_s[:, pl.ds(row0, tm)] = lax.dot_general(
        wvt_ref[...], h, _NT, preferred_element_type=F32).astype(BF16)
    sz_s[...] = _silu(lax.dot_general(wzt_ref[...], h, _NT, preferred_element_type=F32))

    qms = jnp.mean(qt3 * qt3, axis=1, keepdims=True)
    qgain = (qg_ref[...] * (FX_HD ** -0.5 * LOG2E)).reshape(nh, FX_HD, 1)
    qtn = (qt3 * lax.rsqrt(qms + EPS) * qgain).astype(BF16)
    zeros = jnp.zeros((FX_HD, tm), BF16)
    for hd in range(nh):
        halves = [qtn[hd], zeros] if hd % 2 == 0 else [zeros, qtn[hd]]
        wq_s[hd, :LANES, :] = jnp.concatenate(halves, axis=0)

    ss = jnp.dot((k * k).astype(BF16), seg_ref[...], preferred_element_type=F32)
    r = lax.rsqrt(ss * (1.0 / FX_HD) + EPS)
    r_hi = r.astype(BF16)
    r_lo = (r - r_hi.astype(F32)).astype(BF16)
    rr = (jnp.dot(r_hi, segt_ref[...], preferred_element_type=F32)
          + jnp.dot(r_lo, segt_ref[...], preferred_element_type=F32))
    kn = (k * rr * kg_ref[...]).astype(BF16)
    for p in range(npair):
        k_s[p, pl.ds(row0, tm), :] = kn[:, LANES * p:LANES * (p + 1)]

    lane = lax.broadcasted_iota(jnp.int32, (tm, LANES), 1)
    lf = jnp.where(lane < nh, _log_sigmoid(fl + bf_ref[...]), 0.0)
    hi, mid, lo = _split3(lf)
    cum = jnp.dot(tri3_ref[...], jnp.concatenate([hi, mid, lo], axis=0),
                  preferred_element_type=F32)
    cum = cum + jnp.where(i == 0, 0.0, carry_s[0:1, :])
    carry_s[...] = jnp.broadcast_to(cum[tm - 1:tm, :], carry_s.shape)
    chi, cmid, clo = _split3(cum * LOG2E)
    cp = (chi.astype(F32) + pltpu.roll(cmid.astype(F32), nh, axis=1)
          + pltpu.roll(clo.astype(F32), 2 * nh, axis=1))
    if fixed_ref:
        cp = cp + jnp.where((lane >= 3 * nh) & (lane < 3 * nh + 3), 1.0, 0.0)
        ref = jnp.transpose(cum * LOG2E)[:nh, :] - bound_ref[...]
        sub16 = lax.broadcasted_iota(jnp.int32, (16, tm), 0)
        ref_pieces = [p.astype(F32) for p in _split3(ref)]
        for hd in range(nh):
            rhi, rmid, rlo = [p[hd:hd + 1, :] for p in ref_pieces]
            blk = jnp.where(sub16 == 0, rhi, jnp.where(sub16 == 1, rmid,
                                                       jnp.where(sub16 == 2, rlo, 0.0)))
            wq_s[hd, LANES + 3 * nh:LANES + 3 * nh + 16, :] = blk.astype(BF16)
    else:
        m_s[...] = jnp.full(m_s.shape, -jnp.inf, F32)
    cp_s[pl.ds(row0, tm), :] = cp.astype(BF16)

    l_s[...] = jnp.zeros_like(l_s)
    acc_s[...] = jnp.zeros_like(acc_s)
    key_i = lax.broadcasted_iota(jnp.int32, (tk, tm), 0)
    qry_i = lax.broadcasted_iota(jnp.int32, (tk, tm), 1)
    causal = key_i <= qry_i

    def score_fn(j):
        rows = pl.ds(pl.multiple_of(j * tk, tk), tk)
        lhs = {}

        def score(hd):
            p = hd // 2
            if p not in lhs:
                lhs[p] = jnp.concatenate([k_s[p, rows, :], cp_s[rows, :]], axis=1)
            return jnp.dot(lhs[p], wq_s[hd], preferred_element_type=F32)
        return score

    def tile_update(j, ahead, last):
        rows = pl.ds(pl.multiple_of(j * tk, tk), tk)
        score, score_next = score_fn(j), score_fn(j + 1)
        scores = dict(enumerate(ahead))
        ahead_next = []
        for hd in range(nh):
            nxt = hd + FX_LOOKAHEAD
            if nxt < nh:
                scores[nxt] = score(nxt)
            elif not last:
                ahead_next.append(score_next(nxt - nh))
            st = scores.pop(hd)
            if last:
                st = jnp.where(causal, st, -jnp.inf)
            st = st.reshape(tk // 8, 8, tm)
            hs = slice(FX_HD * hd, FX_HD * (hd + 1))
            if fixed_ref:
                pr = jnp.exp2(st)
                l_s[hd] = l_s[hd] + jnp.sum(pr, axis=0)
                acc_s[hs, :] = acc_s[hs, :] + jnp.dot(
                    vt_s[hs, rows], pr.reshape(tk, tm).astype(BF16), preferred_element_type=F32)
                continue
            m_old = m_s[hd]
            mx = jnp.max(st, axis=0)
            mx = jnp.broadcast_to(jnp.max(mx, axis=0, keepdims=True), (8, tm))
            m_new = jnp.maximum(m_old, mx)
            alpha = jnp.exp2(m_old - m_new)
            pr = jnp.exp2(st - m_new)
            m_s[hd] = m_new
            l_s[hd] = alpha * l_s[hd] + jnp.sum(pr, axis=0)
            pv = jnp.dot(vt_s[hs, rows], pr.reshape(tk, tm).astype(BF16),
                         preferred_element_type=F32)
            acc_s[hs, :] = jnp.tile(alpha, (FX_HD // 8, 1)) * acc_s[hs, :] + pv
        return tuple(ahead_next)

    first = score_fn(0)
    ahead = lax.fori_loop(
        0, i // 2,
        lambda jj, ahead: tile_update(2 * jj + 1, tile_update(2 * jj, ahead, False), False),
        tuple(first(hd) for hd in range(FX_LOOKAHEAD)))
    ahead = lax.cond(i % 2 == 1, lambda ahead: tile_update(i - 1, ahead, False),
                     lambda ahead: ahead, ahead)
    tile_update(i, ahead, True)

    inv_l = 1.0 / jnp.sum(l_s[...], axis=1, keepdims=True)
    y = None
    for g in range(d // (2 * LANES)):
        rows_g = slice(2 * LANES * g, 2 * LANES * (g + 1))
        heads_g = range(2 * LANES // FX_HD * g, 2 * LANES // FX_HD * (g + 1))
        ot = jnp.concatenate(
            [acc_s[FX_HD * hd:FX_HD * (hd + 1), :] * inv_l[hd] for hd in heads_g], axis=0)
        on = (ot * sz_s[rows_g, :]).astype(BF16)
        y_g = lax.dot_general(on, wout_ref[rows_g, :], _TN, preferred_element_type=F32)
        y = y_g if y is None else y + y_g
    o_ref[0] = x + gate * y


def _fox_layer(x, mod, norm_g, w_in, b_f, q_g, k_g, w_out):
    bsz, seq, d = x.shape
    tm = FX_SEQ_TILE
    nh = d // FX_HD
    npair = nh // 2
    wq, wk, wv, wz, wf = (w_in[:, :d], w_in[:, d:2 * d], w_in[:, 2 * d:3 * d],
                          w_in[:, 3 * d:4 * d], w_in[:, 4 * d:])
    head_of = jnp.arange(d) // FX_HD
    seg = (head_of[:, None] == jnp.arange(LANES)[None, :]).astype(BF16)
    tri = (jnp.arange(tm)[None, :] <= jnp.arange(tm)[:, None]).astype(BF16)
    tri3 = jnp.concatenate([tri, tri, tri], axis=1)
    sel_row = jnp.arange(LANES)[None, :, None]
    sel_head = jnp.arange(nh)[:, None, None]
    negsel = -((sel_row % nh == sel_head) & (sel_row < 3 * nh)).astype(BF16)
    negsel = jnp.broadcast_to(negsel, (nh, LANES, tm))
    wf_pad = jnp.pad(wf, ((0, 0), (0, LANES - nh))).astype(BF16)
    bf_pad = jnp.pad(b_f, (0, LANES - nh)).reshape(1, LANES)
    bound = 1.01 * FX_HD ** 0.5 * LOG2E * jnp.max(jnp.abs(q_g)) * jnp.max(jnp.abs(k_g))
    operands = (x, mod, norm_g.reshape(1, d), wq.T.astype(BF16), wk.astype(BF16), wv.T.astype(BF16),
                wz.T.astype(BF16), wf_pad, bf_pad, jnp.tile(q_g, nh).reshape(d, 1),
                jnp.tile(k_g, nh).reshape(1, d), seg, seg.T, tri3, negsel, bound.reshape(1, 1),
                w_out.astype(BF16))
    return lax.cond(2.0 * bound <= FX_FIXED_REF_SPAN,
                    functools.partial(_fox_call, True), functools.partial(_fox_call, False),
                    *operands)


def _fox_call(fixed_ref, *operands):
    bsz, seq, d = operands[0].shape
    tm = FX_SEQ_TILE
    nh = d // FX_HD
    npair = nh // 2
    return pl.pallas_call(
        functools.partial(_fox_kernel, fixed_ref),
        grid=(bsz, seq // tm),
        in_specs=[
            pl.BlockSpec((1, tm, d), lambda b, i: (b, i, 0)),
            pl.BlockSpec((1, 3, d), lambda b, i: (b, 0, 0)),
            _const_spec((1, d)),
            _const_spec((d, d)),
            _const_spec((d, d)),
            _const_spec((d, d)),
            _const_spec((d, d)),
            _const_spec((d, LANES)),
            _const_spec((1, LANES)),
            _const_spec((d, 1)),
            _const_spec((1, d)),
            _const_spec((d, LANES)),
            _const_spec((LANES, d)),
            _const_spec((tm, 3 * tm)),
            _const_spec((nh, LANES, tm)),
            _const_spec((1, 1)),
            _const_spec((d, d)),
        ],
        out_specs=pl.BlockSpec((1, tm, d), lambda b, i: (b, i, 0)),
        out_shape=jax.ShapeDtypeStruct(operands[0].shape, F32),
        scratch_shapes=[
            pltpu.VMEM((npair, seq, LANES), BF16),
            pltpu.VMEM((seq, LANES), BF16),
            pltpu.VMEM((d, seq), BF16),
            pltpu.VMEM((8, LANES), F32),
            pltpu.VMEM((d, tm), F32),
            pltpu.VMEM((nh, 2 * LANES, tm), BF16),
            pltpu.VMEM((nh, 8, tm), F32),
            pltpu.VMEM((nh, 8, tm), F32),
            pltpu.VMEM((d, tm), F32),
        ],
        compiler_params=pltpu.CompilerParams(
            dimension_semantics=("arbitrary", "arbitrary"), vmem_limit_bytes=VMEM_LIMIT_BYTES),
        name="fox_layer_fixed_ref" if fixed_ref else "fox_layer_running_max",
    )(*operands)


def kernel(x, c, norm_g, ada_w, ada_b, hg_lb_logits, hg_w_in, hg_o_g, hg_w_out,
           fx_w_in, fx_b_f, fx_q_g, fx_k_g, fx_w_out):
    depth = norm_g.shape[0]
    bsz, _, d = x.shape
    lb_all = jnp.cumsum(jax.nn.softmax(hg_lb_logits.astype(F32), axis=0), axis=0)
    lb_all = lb_all - lb_all[0:1]
    mod = _adaln(c, ada_w, ada_b).reshape(depth, bsz, 3, d)
    for layer in range(depth):
        j = layer // 2
        if layer % 2 == 0:
            x = _hgrn_layer(x, mod[layer], norm_g[layer], hg_w_in[j], lb_all[j], hg_o_g[j],
                            hg_w_out[j], lb_is_zero=(j == 0))
        else:
            x = _fox_layer(x, mod[layer], norm_g[layer], fx_w_in[j], fx_b_f[j], fx_q_g[j],
                           fx_k_g[j], fx_w_out[j])
    return x
```

```python
import functools

import jax
import jax.numpy as jnp
from jax import lax
from jax.experimental import pallas as pl
from jax.experimental.pallas import tpu as pltpu

F32 = jnp.float32
BF16 = jnp.bfloat16

EPS = 1e-6
LOG2E = 1.4426950408889634
LANES = 128
HG_DK = 128
HG_CHUNK = 64
HG_LEVELS = (32, 16, 8, 4, 2, 1, 0)
HG_GROUP = 8
FX_HD = 64
FX_LOOKAHEAD = 4
FX_FIXED_REF_SPAN = 96.0
HG_SEQ_TILE = 512
FX_SEQ_TILE = 256
VMEM_LIMIT_BYTES = 56 * 1024 * 1024

_NT = (((1,), (1,)), ((), ()))
_TN = (((0,), (0,)), ((), ()))


def _split3(x):
    hi = x.astype(BF16)
    r1 = x - hi.astype(F32)
    mid = r1.astype(BF16)
    lo = (r1 - mid.astype(F32)).astype(BF16)
    return hi, mid, lo


def _modulated_norm(x, g, scale, shift):
    ms = jnp.mean(x * x, axis=-1, keepdims=True)
    return (x * lax.rsqrt(ms + EPS) * g) * (1.0 + scale) + shift


def _silu(x):
    return x / (1.0 + jnp.exp(-x))


def _log_sigmoid(x):
    return jnp.minimum(x, 0.0) - jnp.log(1.0 + jnp.exp(-jnp.abs(x)))


def _adaln_kernel(c_ref, w_ref, b_ref, o_ref):
    c_act = _silu(c_ref[...])
    o_ref[0] = jnp.dot(c_act, w_ref[0], precision=lax.Precision.HIGHEST,
                       preferred_element_type=F32) + b_ref[0]


def _adaln(c, ada_w, ada_b):
    depth, d, n = ada_w.shape
    b = c.shape[0]
    tn = 1024
    return pl.pallas_call(
        _adaln_kernel,
        grid=(depth, n // tn),
        in_specs=[
            pl.BlockSpec((b, d), lambda l, j: (0, 0)),
            pl.BlockSpec((1, d, tn), lambda l, j: (l, 0, j)),
            pl.BlockSpec((1, 1, tn), lambda l, j: (l, 0, j)),
        ],
        out_specs=pl.BlockSpec((1, b, tn), lambda l, j: (l, 0, j)),
        out_shape=jax.ShapeDtypeStruct((depth, b, n), F32),
        compiler_params=pltpu.CompilerParams(
            dimension_semantics=("arbitrary", "arbitrary"), vmem_limit_bytes=VMEM_LIMIT_BYTES),
        name="adaln_mod",
    )(c, ada_w, ada_b.reshape(depth, 1, n))


def _hgrn_level_operands(q, k, b, lf, half, sub3):
    c = HG_CHUNK
    if half == 0:
        return q, k
    if half >= 8:
        qp, kp = [], []
        for g in range(c // (2 * half)):
            lo_r = slice(2 * half * g, 2 * half * g + half)
            up_r = slice(2 * half * g + half, 2 * half * (g + 1))
            ref = b[2 * half * g + half - 1:2 * half * g + half, :]
            zeros = jnp.zeros((half, HG_DK), F32)
            qp += [zeros, q[up_r] * jnp.exp2(b[up_r] - ref)]
            kp += [k[lo_r] * jnp.exp2(ref - b[lo_r]), zeros]
        return jnp.concatenate(qp, axis=0), jnp.concatenate(kp, axis=0)
    shape3 = (c // 8, 8, HG_DK)
    q3, k3, b3 = q.reshape(shape3), k.reshape(shape3), b.reshape(shape3)
    if half == 1:
        upper = sub3 % 2 == 1
        qp = jnp.where(upper, q3 * jnp.exp2(lf.reshape(shape3)), 0.0)
        kp = jnp.where(upper, 0.0, k3)
    else:
        if half == 4:
            ref = b3[:, 3:4, :]
        else:
            ref = jnp.where(sub3 < 4, b3[:, 1:2, :], b3[:, 5:6, :])
        upper = sub3 % (2 * half) >= half
        w = jnp.exp2(-jnp.abs(b3 - ref))
        qp = jnp.where(upper, q3 * w, 0.0)
        kp = jnp.where(upper, 0.0, k3 * w)
    return qp.reshape(c, HG_DK), kp.reshape(c, HG_DK)


def _blockdiag(a, b):
    z = jnp.zeros_like(a)
    return jnp.concatenate([jnp.concatenate([a, z], axis=1), jnp.concatenate([z, b], axis=1)],
                           axis=0)


def _hgrn_heads(qs, lfs, ks, vs, states, tri3, level_masks, sub3):
    c = HG_CHUNK
    n = len(qs)
    per_head = n // len(states)
    pairs = [(i, i + 1) for i in range(0, n, 2)]
    lo, hi = slice(0, HG_DK), slice(HG_DK, 2 * HG_DK)
    bs = [None] * n
    for a, b in pairs:
        rhs = jnp.concatenate([jnp.concatenate(_split3(lfs[a]), axis=0),
                               jnp.concatenate(_split3(lfs[b]), axis=0)], axis=1)
        b2 = jnp.dot(tri3, rhs, preferred_element_type=F32)
        bs[a], bs[b] = b2[:, lo], b2[:, hi]
    vbs = [v.astype(BF16) for v in vs]
    o_intra = [None] * n
    a_sums = []
    for a, b in pairs:
        a_sum = None
        for half, mask in zip(HG_LEVELS, level_masks):
            qa, ka = _hgrn_level_operands(qs[a], ks[a], bs[a], lfs[a], half, sub3)
            qb, kb = _hgrn_level_operands(qs[b], ks[b], bs[b], lfs[b], half, sub3)
            a_l = lax.dot_general(jnp.concatenate([qa, qb], axis=1).astype(BF16),
                                  _blockdiag(ka.astype(BF16), kb.astype(BF16)), _NT,
                                  preferred_element_type=F32)
            a_l = a_l if mask is None else jnp.where(mask, a_l, 0.0)
            a_sum = a_l if a_sum is None else a_sum + a_l
        a_sums.append(a_sum)
    for (a, b), a_sum in zip(pairs, a_sums):
        o2 = jnp.dot(a_sum.astype(BF16), _blockdiag(vbs[a], vbs[b]), preferred_element_type=F32)
        o_intra[a], o_intra[b] = o2[:, lo], o2[:, hi]
    blasts = [b[c - 1:c, :] for b in bs]
    q1s = [(qs[i] * jnp.exp2(bs[i])).astype(BF16) for i in range(n)]
    kds = [(ks[i] * jnp.exp2(blasts[i] - bs[i])).astype(BF16) for i in range(n)]
    pcs = [None] * n
    for a, b in pairs:
        p2 = lax.dot_general(jnp.concatenate([vbs[a], vbs[b]], axis=0), _blockdiag(kds[a], kds[b]),
                             _TN, preferred_element_type=F32)
        pcs[a], pcs[b] = p2[:, lo], p2[:, hi]
    st_in, st_out = [], []
    for h, st in enumerate(states):
        for i in range(per_head * h, per_head * (h + 1)):
            st_in.append(st)
            st = st * jnp.exp2(blasts[i]) + pcs[i]
        st_out.append(st)
    outs = [None] * n
    for a, b in pairs:
        i2 = lax.dot_general(jnp.concatenate([q1s[a], q1s[b]], axis=1),
                             _blockdiag(st_in[a].astype(BF16), st_in[b].astype(BF16)), _NT,
                             preferred_element_type=F32)
        outs[a], outs[b] = o_intra[a] + i2[:, lo], o_intra[b] + i2[:, hi]
    return outs, st_out


def _hgrn_kernel(lb_is_zero, x_ref, mod_ref, ng_ref, win_ref, lb_ref, og_ref, wout_ref, o_ref,
                 st_ref, q_s, lf_s, k_s, v_s, z_s, oh_s):
    i = pl.program_id(1)
    tm = x_ref.shape[1]
    d = x_ref.shape[2]
    nh = d // HG_DK
    c = HG_CHUNK

    @pl.when(i == 0)
    def _():
        st_ref[...] = jnp.zeros_like(st_ref)

    x = x_ref[0]
    shift, scale, gate = mod_ref[0, 0:1, :], mod_ref[0, 1:2, :], mod_ref[0, 2:3, :]
    h = _modulated_norm(x, ng_ref[...], scale, shift).astype(BF16)

    if not lb_is_zero:
        lb = lb_ref[...]
        l0, l1, oml = jnp.log(lb), jnp.log1p(-lb), 1.0 - lb
    blocks = [(sect, hp) for hp in range(nh // 2) for sect in range(3)]
    blocks += [(3, hp) for hp in range(nh // 2)]
    for sect, hp in blocks:
        blk = sect * (nh // 2) + hp
        p = jnp.dot(h, win_ref[:, 256 * blk:256 * (blk + 1)], preferred_element_type=F32)
        for hh in range(2):
            head = 2 * hp + hh
            ph = p[:, HG_DK * hh:HG_DK * (hh + 1)]
            if sect == 0:
                q_s[head] = ph
            elif sect == 1:
                lbh = slice(HG_DK * head, HG_DK * (head + 1))
                e = jnp.exp(-jnp.abs(ph))
                sig_neg = jnp.where(ph >= 0.0, e, 1.0) / (1.0 + e)
                if lb_is_zero:
                    lf_s[head] = jnp.minimum(ph, 0.0) * LOG2E - jnp.log2(1.0 + e)
                    k_s[head] = sig_neg
                else:
                    cc = l1[:, lbh] + (jnp.minimum(ph, 0.0) - jnp.log(1.0 + e))
                    hi_arg = jnp.maximum(l0[:, lbh], cc)
                    lf_s[head] = (hi_arg + jnp.log(1.0 + jnp.exp(-jnp.abs(l0[:, lbh] - cc)))) * LOG2E
                    k_s[head] = oml[:, lbh] * sig_neg
            elif sect == 2:
                v_s[head] = ph
            else:
                z_s[:, HG_DK * head:HG_DK * (head + 1)] = _silu(ph)

    ri = lax.broadcasted_iota(jnp.int32, (c, 3 * c), 0)
    ci = lax.broadcasted_iota(jnp.int32, (c, 3 * c), 1)
    tri3 = ((ci % c) <= ri).astype(BF16)
    ti = lax.broadcasted_iota(jnp.int32, (c, 2 * c), 0)
    si = lax.broadcasted_iota(jnp.int32, (c, 2 * c), 1) % c
    level_masks = [None if 2 * half == c else
                   (ti == si if half == 0 else (ti // (2 * half)) == (si // (2 * half)))
                   for half in HG_LEVELS]
    sub3 = lax.broadcasted_iota(jnp.int32, (1, 8, HG_DK), 1)
    chunks = [slice(c * ch, c * (ch + 1)) for ch in range(tm // c)]

    def group_body(g, carry):
        heads = [g * HG_GROUP + hh for hh in range(HG_GROUP)]
        flat = [(hd, r) for hd in heads for r in chunks]
        outs, states = _hgrn_heads([q_s[hd, r, :] for hd, r in flat],
                                   [lf_s[hd, r, :] for hd, r in flat],
                                   [k_s[hd, r, :] for hd, r in flat],
                                   [v_s[hd, r, :] for hd, r in flat],
                                   [st_ref[hd] for hd in heads], tri3, level_masks, sub3)
        for (hd, r), o_c in zip(flat, outs):
            oh_s[hd, r, :] = o_c
        for hd, st in zip(heads, states):
            st_ref[hd] = st
        return carry

    lax.fori_loop(0, nh // HG_GROUP, group_body, 0)

    og = og_ref[...]
    y = None
    for hp in range(nh // 2):
        outs = []
        for head in (2 * hp, 2 * hp + 1):
            oh = oh_s[head]
            ms = jnp.mean(oh * oh, axis=-1, keepdims=True)
            outs.append(oh * lax.rsqrt(ms + EPS) * og[:, HG_DK * head:HG_DK * (head + 1)])
        cols = slice(2 * HG_DK * hp, 2 * HG_DK * (hp + 1))
        on = jnp.concatenate(outs, axis=1) * z_s[:, cols]
        y_p = jnp.dot(on.astype(BF16), wout_ref[cols, :], preferred_element_type=F32)
        y = y_p if y is None else y + y_p
    o_ref[0] = x + gate * y


def _const_spec(shape):
    return pl.BlockSpec(shape, lambda b, i: (0,) * len(shape), pipeline_mode=pl.Buffered(1))


def _hgrn_layer(x, mod, norm_g, w_in, lb, o_g, w_out, lb_is_zero):
    bsz, seq, d = x.shape
    tm = HG_SEQ_TILE
    nh = d // HG_DK
    slab = pltpu.VMEM((nh, tm, HG_DK), F32)
    return pl.pallas_call(
        functools.partial(_hgrn_kernel, lb_is_zero),
        grid=(bsz, seq // tm),
        in_specs=[
            pl.BlockSpec((1, tm, d), lambda b, i: (b, i, 0)),
            pl.BlockSpec((1, 3, d), lambda b, i: (b, 0, 0)),
            _const_spec((1, d)),
            _const_spec((d, 4 * d)),
            _const_spec((1, d)),
            _const_spec((1, d)),
            _const_spec((d, d)),
        ],
        out_specs=pl.BlockSpec((1, tm, d), lambda b, i: (b, i, 0)),
        out_shape=jax.ShapeDtypeStruct(x.shape, F32),
        scratch_shapes=[
            pltpu.VMEM((nh, HG_DK, HG_DK), F32),
            slab, slab, slab, slab,
            pltpu.VMEM((tm, d), F32),
            slab,
        ],
        compiler_params=pltpu.CompilerParams(
            dimension_semantics=("arbitrary", "arbitrary"), vmem_limit_bytes=VMEM_LIMIT_BYTES),
        name="hgrn_layer",
    )(x, mod, norm_g.reshape(1, d), w_in.astype(BF16), lb.reshape(1, d), o_g.reshape(1, d),
      w_out.astype(BF16))


def _fox_kernel(fixed_ref, x_ref, mod_ref, ng_ref, wqvzt_ref, wk_ref, wf_ref,
                bf_ref, qg_ref, kg_ref, seg_ref, segt_ref, tri3_ref, negsel_ref, bound_ref, wout_ref,
                o_ref, k_s, cp_s, vt_s, carry_s, sz_s, wq_s, m_s, l_s, acc_s):
    b = pl.program_id(0)
    i = pl.program_id(1)
    tm = x_ref.shape[1]
    d = x_ref.shape[2]
    nh = d // FX_HD
    npair = nh // 2
    tk = tm
    row0 = pl.multiple_of(i * tm, tm)

    @pl.when((b == 0) & (i == 0))
    def _():
        wq_s[:, LANES:, :] = negsel_ref[...]

    x = x_ref[0]
    shift, scale, gate = mod_ref[0, 0:1, :], mod_ref[0, 1:2, :], mod_ref[0, 2:3, :]
    h = _modulated_norm(x, ng_ref[...], scale, shift).astype(BF16)

    qvzt = lax.dot_general(wqvzt_ref[...], h, _NT, preferred_element_type=F32)
    k = jnp.dot(h, wk_ref[...], preferred_element_type=F32)
    fl = jnp.dot(h, wf_ref[...], preferred_element_type=F32)
    qt3 = qvzt[:d].reshape(nh, FX_HD, tm)
    vt_s[:, pl.ds(row0, tm)] = qvzt[d:2 * d].astype(BF16)
    sz_s[...] = _silu(qvzt[2 * d:])

    qms = jnp.mean(qt3 * qt3, axis=1, keepdims=True)
    qgain = (qg_ref[...] * (FX_HD ** -0.5 * LOG2E)).reshape(nh, FX_HD, 1)
    qtn = (qt3 * lax.rsqrt(qms + EPS) * qgain).astype(BF16)
    zeros = jnp.zeros((FX_HD, tm), BF16)
    for hd in range(nh):
        halves = [qtn[hd], zeros] if hd % 2 == 0 else [zeros, qtn[hd]]
        wq_s[hd, :LANES, :] = jnp.concatenate(halves, axis=0)

    ss = jnp.dot((k * k).astype(BF16), seg_ref[...], preferred_element_type=F32)
    r = lax.rsqrt(ss * (1.0 / FX_HD) + EPS)
    r_hi = r.astype(BF16)
    r_lo = (r - r_hi.astype(F32)).astype(BF16)
    rr = (jnp.dot(r_hi, segt_ref[...], preferred_element_type=F32)
          + jnp.dot(r_lo, segt_ref[...], preferred_element_type=F32))
    kn = (k * rr * kg_ref[...]).astype(BF16)
    for p in range(npair):
        k_s[p, pl.ds(row0, tm), :] = kn[:, LANES * p:LANES * (p + 1)]

    lane = lax.broadcasted_iota(jnp.int32, (tm, LANES), 1)
    lf = jnp.where(lane < nh, _log_sigmoid(fl + bf_ref[...]), 0.0)
    hi, mid, lo = _split3(lf)
    cum = jnp.dot(tri3_ref[...], jnp.concatenate([hi, mid, lo], axis=0),
                  preferred_element_type=F32)
    cum = cum + jnp.where(i == 0, 0.0, carry_s[0:1, :])
    carry_s[...] = jnp.broadcast_to(cum[tm - 1:tm, :], carry_s.shape)
    chi, cmid, clo = _split3(cum * LOG2E)
    cp = (chi.astype(F32) + pltpu.roll(cmid.astype(F32), nh, axis=1)
          + pltpu.roll(clo.astype(F32), 2 * nh, axis=1))
    if fixed_ref:
        cp = cp + jnp.where((lane >= 3 * nh) & (lane < 3 * nh + 3), 1.0, 0.0)
        ref = jnp.transpose(cum * LOG2E)[:nh, :] - bound_ref[...]
        sub16 = lax.broadcasted_iota(jnp.int32, (16, tm), 0)
        ref_pieces = [p.astype(F32) for p in _split3(ref)]
        for hd in range(nh):
            rhi, rmid, rlo = [p[hd:hd + 1, :] for p in ref_pieces]
            blk = jnp.where(sub16 == 0, rhi, jnp.where(sub16 == 1, rmid,
                                                       jnp.where(sub16 == 2, rlo, 0.0)))
            wq_s[hd, LANES + 3 * nh:LANES + 3 * nh + 16, :] = blk.astype(BF16)
    else:
        m_s[...] = jnp.full(m_s.shape, -jnp.inf, F32)
    cp_s[pl.ds(row0, tm), :] = cp.astype(BF16)

    l_s[...] = jnp.zeros_like(l_s)
    acc_s[...] = jnp.zeros_like(acc_s)
    key_i = lax.broadcasted_iota(jnp.int32, (tk, tm), 0)
    qry_i = lax.broadcasted_iota(jnp.int32, (tk, tm), 1)
    causal = key_i <= qry_i

    def score_fn(j):
        rows = pl.ds(pl.multiple_of(j * tk, tk), tk)
        lhs = {}

        def score(hd):
            p = hd // 2
            if p not in lhs:
                lhs[p] = jnp.concatenate([k_s[p, rows, :], cp_s[rows, :]], axis=1)
            return jnp.dot(lhs[p], wq_s[hd], preferred_element_type=F32)
        return score

    def tile_update(j, ahead, last):
        rows = pl.ds(pl.multiple_of(j * tk, tk), tk)
        score, score_next = score_fn(j), score_fn(j + 1)
        scores = dict(enumerate(ahead))
        ahead_next = []
        for hd in range(nh):
            nxt = hd + FX_LOOKAHEAD
            if nxt < nh:
                scores[nxt] = score(nxt)
            elif not last:
                ahead_next.append(score_next(nxt - nh))
            st = scores.pop(hd)
            if last:
                st = jnp.where(causal, st, -jnp.inf)
            st = st.reshape(tk // 8, 8, tm)
            hs = slice(FX_HD * hd, FX_HD * (hd + 1))
            if fixed_ref:
                pr = jnp.exp2(st)
                l_s[hd] = l_s[hd] + jnp.sum(pr, axis=0)
                acc_s[hs, :] = acc_s[hs, :] + jnp.dot(
                    vt_s[hs, rows], pr.reshape(tk, tm).astype(BF16), preferred_element_type=F32)
                continue
            m_old = m_s[hd]
            mx = jnp.max(st, axis=0)
            mx = jnp.broadcast_to(jnp.max(mx, axis=0, keepdims=True), (8, tm))
            m_new = jnp.maximum(m_old, mx)
            alpha = jnp.exp2(m_old - m_new)
            pr = jnp.exp2(st - m_new)
            m_s[hd] = m_new
            l_s[hd] = alpha * l_s[hd] + jnp.sum(pr, axis=0)
            pv = jnp.dot(vt_s[hs, rows], pr.reshape(tk, tm).astype(BF16),
                         preferred_element_type=F32)
            acc_s[hs, :] = jnp.tile(alpha, (FX_HD // 8, 1)) * acc_s[hs, :] + pv
        return tuple(ahead_next)

    first = score_fn(0)
    ahead = lax.fori_loop(
        0, i // 2,
        lambda jj, ahead: tile_update(2 * jj + 1, tile_update(2 * jj, ahead, False), False),
        tuple(first(hd) for hd in range(FX_LOOKAHEAD)))
    ahead = lax.cond(i % 2 == 1, lambda ahead: tile_update(i - 1, ahead, False),
                     lambda ahead: ahead, ahead)
    tile_update(i, ahead, True)

    inv_l = 1.0 / jnp.sum(l_s[...], axis=1, keepdims=True)
    y = None
    for g in range(d // (2 * LANES)):
        rows_g = slice(2 * LANES * g, 2 * LANES * (g + 1))
        heads_g = range(2 * LANES // FX_HD * g, 2 * LANES // FX_HD * (g + 1))
        ot = jnp.concatenate(
            [acc_s[FX_HD * hd:FX_HD * (hd + 1), :] * inv_l[hd] for hd in heads_g], axis=0)
        on = (ot * sz_s[rows_g, :]).astype(BF16)
        y_g = lax.dot_general(on, wout_ref[rows_g, :], _TN, preferred_element_type=F32)
        y = y_g if y is None else y + y_g
    o_ref[0] = x + gate * y


def _fox_layer(x, mod, norm_g, w_in, b_f, q_g, k_g, w_out):
    bsz, seq, d = x.shape
    tm = FX_SEQ_TILE
    nh = d // FX_HD
    npair = nh // 2
    wq, wk, wv, wz, wf = (w_in[:, :d], w_in[:, d:2 * d], w_in[:, 2 * d:3 * d],
                          w_in[:, 3 * d:4 * d], w_in[:, 4 * d:])
    head_of = jnp.arange(d) // FX_HD
    seg = (head_of[:, None] == jnp.arange(LANES)[None, :]).astype(BF16)
    tri = (jnp.arange(tm)[None, :] <= jnp.arange(tm)[:, None]).astype(BF16)
    tri3 = jnp.concatenate([tri, tri, tri], axis=1)
    sel_row = jnp.arange(LANES)[None, :, None]
    sel_head = jnp.arange(nh)[:, None, None]
    negsel = -((sel_row % nh == sel_head) & (sel_row < 3 * nh)).astype(BF16)
    negsel = jnp.broadcast_to(negsel, (nh, LANES, tm))
    wf_pad = jnp.pad(wf, ((0, 0), (0, LANES - nh))).astype(BF16)
    bf_pad = jnp.pad(b_f, (0, LANES - nh)).reshape(1, LANES)
    bound = 1.01 * FX_HD ** 0.5 * LOG2E * jnp.max(jnp.abs(q_g)) * jnp.max(jnp.abs(k_g))
    operands = (x, mod, norm_g.reshape(1, d), jnp.concatenate([wq, wv, wz], axis=1).T.astype(BF16),
                wk.astype(BF16), wf_pad, bf_pad, jnp.tile(q_g, nh).reshape(d, 1),
                jnp.tile(k_g, nh).reshape(1, d), seg, seg.T, tri3, negsel, bound.reshape(1, 1),
                w_out.astype(BF16))
    return lax.cond(2.0 * bound <= FX_FIXED_REF_SPAN,
                    functools.partial(_fox_call, True), functools.partial(_fox_call, False),
                    *operands)


def _fox_call(fixed_ref, *operands):
    bsz, seq, d = operands[0].shape
    tm = FX_SEQ_TILE
    nh = d // FX_HD
    npair = nh // 2
    return pl.pallas_call(
        functools.partial(_fox_kernel, fixed_ref),
        grid=(bsz, seq // tm),
        in_specs=[
            pl.BlockSpec((1, tm, d), lambda b, i: (b, i, 0)),
            pl.BlockSpec((1, 3, d), lambda b, i: (b, 0, 0)),
            _const_spec((1, d)),
            _const_spec((3 * d, d)),
            _const_spec((d, d)),
            _const_spec((d, LANES)),
            _const_spec((1, LANES)),
            _const_spec((d, 1)),
            _const_spec((1, d)),
            _const_spec((d, LANES)),
            _const_spec((LANES, d)),
            _const_spec((tm, 3 * tm)),
            _const_spec((nh, LANES, tm)),
            _const_spec((1, 1)),
            _const_spec((d, d)),
        ],
        out_specs=pl.BlockSpec((1, tm, d), lambda b, i: (b, i, 0)),
        out_shape=jax.ShapeDtypeStruct(operands[0].shape, F32),
        scratch_shapes=[
            pltpu.VMEM((npair, seq, LANES), BF16),
            pltpu.VMEM((seq, LANES), BF16),
            pltpu.VMEM((d, seq), BF16),
            pltpu.VMEM((8, LANES), F32),
            pltpu.VMEM((d, tm), F32),
            pltpu.VMEM((nh, 2 * LANES, tm), BF16),
            pltpu.VMEM((nh, 8, tm), F32),
            pltpu.VMEM((nh, 8, tm), F32),
            pltpu.VMEM((d, tm), F32),
        ],
        compiler_params=pltpu.CompilerParams(
            dimension_semantics=("arbitrary", "arbitrary"), vmem_limit_bytes=VMEM_LIMIT_BYTES),
        name="fox_layer_fixed_ref" if fixed_ref else "fox_layer_running_max",
    )(*operands)


def kernel(x, c, norm_g, ada_w, ada_b, hg_lb_logits, hg_w_in, hg_o_g, hg_w_out,
           fx_w_in, fx_b_f, fx_q_g, fx_k_g, fx_w_out):
    depth = norm_g.shape[0]
    bsz, _, d = x.shape
    lb_all = jnp.cumsum(jax.nn.softmax(hg_lb_logits.astype(F32), axis=0), axis=0)
    lb_all = lb_all - lb_all[0:1]
    mod = _adaln(c, ada_w, ada_b).reshape(depth, bsz, 3, d)
    for layer in range(depth):
        j = layer // 2
        if layer % 2 == 0:
            x = _hgrn_layer(x, mod[layer], norm_g[layer], hg_w_in[j], lb_all[j], hg_o_g[j],
                            hg_w_out[j], lb_is_zero=(j == 0))
        else:
            x = _fox_layer(x, mod[layer], norm_g[layer], fx_w_in[j], fx_b_f[j], fx_q_g[j],
                           fx_k_g[j], fx_w_out[j])
    return x
```

```python
import functools

import jax
import jax.numpy as jnp
from jax import lax
from jax.experimental import pallas as pl
from jax.experimental.pallas import tpu as pltpu

F32 = jnp.float32
BF16 = jnp.bfloat16

EPS = 1e-6
LOG2E = 1.4426950408889634
LANES = 128
HG_DK = 128
HG_CHUNK = 64
HG_LEVELS = (32, 16, 8, 4, 2, 1, 0)
HG_GROUP = 8
FX_HD = 64
FX_LOOKAHEAD = 4
FX_FIXED_REF_SPAN = 96.0
HG_SEQ_TILE = 512
FX_SEQ_TILE = 256
VMEM_LIMIT_BYTES = 56 * 1024 * 1024

_NT = (((1,), (1,)), ((), ()))
_TN = (((0,), (0,)), ((), ()))


def _split3(x):
    hi = x.astype(BF16)
    r1 = x - hi.astype(F32)
    mid = r1.astype(BF16)
    lo = (r1 - mid.astype(F32)).astype(BF16)
    return hi, mid, lo


def _modulated_norm(x, g, scale, shift):
    ms = jnp.mean(x * x, axis=-1, keepdims=True)
    return (x * lax.rsqrt(ms + EPS) * g) * (1.0 + scale) + shift


def _silu(x):
    return x / (1.0 + jnp.exp(-x))


def _log_sigmoid(x):
    return jnp.minimum(x, 0.0) - jnp.log(1.0 + jnp.exp(-jnp.abs(x)))


def _adaln_kernel(c_ref, w_ref, b_ref, o_ref):
    c_act = _silu(c_ref[...])
    o_ref[0] = jnp.dot(c_act, w_ref[0], precision=lax.Precision.HIGHEST,
                       preferred_element_type=F32) + b_ref[0]


def _adaln(c, ada_w, ada_b):
    depth, d, n = ada_w.shape
    b = c.shape[0]
    tn = 1024
    return pl.pallas_call(
        _adaln_kernel,
        grid=(depth, n // tn),
        in_specs=[
            pl.BlockSpec((b, d), lambda l, j: (0, 0)),
            pl.BlockSpec((1, d, tn), lambda l, j: (l, 0, j)),
            pl.BlockSpec((1, 1, tn), lambda l, j: (l, 0, j)),
        ],
        out_specs=pl.BlockSpec((1, b, tn), lambda l, j: (l, 0, j)),
        out_shape=jax.ShapeDtypeStruct((depth, b, n), F32),
        compiler_params=pltpu.CompilerParams(
            dimension_semantics=("arbitrary", "arbitrary"), vmem_limit_bytes=VMEM_LIMIT_BYTES),
        name="adaln_mod",
    )(c, ada_w, ada_b.reshape(depth, 1, n))


def _hgrn_level_operands(q, k, b, lf, half, sub3):
    c = HG_CHUNK
    if half == 0:
        return q, k
    if half >= 8:
        qp, kp = [], []
        for g in range(c // (2 * half)):
            lo_r = slice(2 * half * g, 2 * half * g + half)
            up_r = slice(2 * half * g + half, 2 * half * (g + 1))
            ref = b[2 * half * g + half - 1:2 * half * g + half, :]
            zeros = jnp.zeros((half, HG_DK), F32)
            qp += [zeros, q[up_r] * jnp.exp2(b[up_r] - ref)]
            kp += [k[lo_r] * jnp.exp2(ref - b[lo_r]), zeros]
        return jnp.concatenate(qp, axis=0), jnp.concatenate(kp, axis=0)
    shape3 = (c // 8, 8, HG_DK)
    q3, k3, b3 = q.reshape(shape3), k.reshape(shape3), b.reshape(shape3)
    if half == 1:
        upper = sub3 % 2 == 1
        qp = jnp.where(upper, q3 * jnp.exp2(lf.reshape(shape3)), 0.0)
        kp = jnp.where(upper, 0.0, k3)
    else:
        if half == 4:
            ref = b3[:, 3:4, :]
        else:
            ref = jnp.where(sub3 < 4, b3[:, 1:2, :], b3[:, 5:6, :])
        upper = sub3 % (2 * half) >= half
        w = jnp.exp2(-jnp.abs(b3 - ref))
        qp = jnp.where(upper, q3 * w, 0.0)
        kp = jnp.where(upper, 0.0, k3 * w)
    return qp.reshape(c, HG_DK), kp.reshape(c, HG_DK)


def _blockdiag(a, b):
    z = jnp.zeros_like(a)
    return jnp.concatenate([jnp.concatenate([a, z], axis=1), jnp.concatenate([z, b], axis=1)],
                           axis=0)


def _hgrn_heads(qs, lfs, ks, vs, states, tri3, level_masks, sub3):
    c = HG_CHUNK
    n = len(qs)
    per_head = n // len(states)
    pairs = [(i, i + 1) for i in range(0, n, 2)]
    lo, hi = slice(0, HG_DK), slice(HG_DK, 2 * HG_DK)
    bs = [None] * n
    for a, b in pairs:
        rhs = jnp.concatenate([jnp.concatenate(_split3(lfs[a]), axis=0),
                               jnp.concatenate(_split3(lfs[b]), axis=0)], axis=1)
        b2 = jnp.dot(tri3, rhs, preferred_element_type=F32)
        bs[a], bs[b] = b2[:, lo], b2[:, hi]
    vbs = [v.astype(BF16) for v in vs]
    o_intra = [None] * n
    a_sums = []
    for a, b in pairs:
        a_sum = None
        for half, mask in zip(HG_LEVELS, level_masks):
            qa, ka = _hgrn_level_operands(qs[a], ks[a], bs[a], lfs[a], half, sub3)
            qb, kb = _hgrn_level_operands(qs[b], ks[b], bs[b], lfs[b], half, sub3)
            a_l = lax.dot_general(jnp.concatenate([qa, qb], axis=1).astype(BF16),
                                  _blockdiag(ka.astype(BF16), kb.astype(BF16)), _NT,
                                  preferred_element_type=F32)
            a_l = a_l if mask is None else jnp.where(mask, a_l, 0.0)
            a_sum = a_l if a_sum is None else a_sum + a_l
        a_sums.append(a_sum)
    for (a, b), a_sum in zip(pairs, a_sums):
        o2 = jnp.dot(a_sum.astype(BF16), _blockdiag(vbs[a], vbs[b]), preferred_element_type=F32)
        o_intra[a], o_intra[b] = o2[:, lo], o2[:, hi]
    blasts = [b[c - 1:c, :] for b in bs]
    q1s = [(qs[i] * jnp.exp2(bs[i])).astype(BF16) for i in range(n)]
    kds = [(ks[i] * jnp.exp2(blasts[i] - bs[i])).astype(BF16) for i in range(n)]
    pcs = [None] * n
    for a, b in pairs:
        p2 = lax.dot_general(jnp.concatenate([vbs[a], vbs[b]], axis=0), _blockdiag(kds[a], kds[b]),
                             _TN, preferred_element_type=F32)
        pcs[a], pcs[b] = p2[:, lo], p2[:, hi]
    st_in, st_out = [], []
    for h, st in enumerate(states):
        for i in range(per_head * h, per_head * (h + 1)):
            st_in.append(st)
            st = st * jnp.exp2(blasts[i]) + pcs[i]
        st_out.append(st)
    outs = [None] * n
    for a, b in pairs:
        i2 = lax.dot_general(jnp.concatenate([q1s[a], q1s[b]], axis=1),
                             _blockdiag(st_in[a].astype(BF16), st_in[b].astype(BF16)), _NT,
                             preferred_element_type=F32)
        outs[a], outs[b] = o_intra[a] + i2[:, lo], o_intra[b] + i2[:, hi]
    return outs, st_out


def _hgrn_kernel(lb_is_zero, x_ref, mod_ref, ng_ref, win_ref, lb_ref, og_ref, wout_ref, o_ref,
                 st_ref, q_s, lf_s, k_s, v_s, z_s):
    i = pl.program_id(1)
    tm = x_ref.shape[1]
    d = x_ref.shape[2]
    nh = d // HG_DK
    c = HG_CHUNK

    @pl.when(i == 0)
    def _():
        st_ref[...] = jnp.zeros_like(st_ref)

    x = x_ref[0]
    shift, scale, gate = mod_ref[0, 0:1, :], mod_ref[0, 1:2, :], mod_ref[0, 2:3, :]
    h = _modulated_norm(x, ng_ref[...], scale, shift).astype(BF16)

    if not lb_is_zero:
        lb = lb_ref[...]
        l0, l1, oml = jnp.log(lb), jnp.log1p(-lb), 1.0 - lb
    def project(sect, hp):
        blk = sect * (nh // 2) + hp
        p = jnp.dot(h, win_ref[:, 256 * blk:256 * (blk + 1)], preferred_element_type=F32)
        for hh in range(2):
            head = 2 * hp + hh
            ph = p[:, HG_DK * hh:HG_DK * (hh + 1)]
            if sect == 0:
                q_s[head] = ph
            elif sect == 1:
                lbh = slice(HG_DK * head, HG_DK * (head + 1))
                e = jnp.exp(-jnp.abs(ph))
                sig_neg = jnp.where(ph >= 0.0, e, 1.0) / (1.0 + e)
                if lb_is_zero:
                    lf_s[head] = jnp.minimum(ph, 0.0) * LOG2E - jnp.log2(1.0 + e)
                    k_s[head] = sig_neg
                else:
                    cc = l1[:, lbh] + (jnp.minimum(ph, 0.0) - jnp.log(1.0 + e))
                    hi_arg = jnp.maximum(l0[:, lbh], cc)
                    lf_s[head] = (hi_arg + jnp.log(1.0 + jnp.exp(-jnp.abs(l0[:, lbh] - cc)))) * LOG2E
                    k_s[head] = oml[:, lbh] * sig_neg
            elif sect == 2:
                v_s[head] = ph
            else:
                z_s[:, HG_DK * head:HG_DK * (head + 1)] = _silu(ph)

    ri = lax.broadcasted_iota(jnp.int32, (c, 3 * c), 0)
    ci = lax.broadcasted_iota(jnp.int32, (c, 3 * c), 1)
    tri3 = ((ci % c) <= ri).astype(BF16)
    ti = lax.broadcasted_iota(jnp.int32, (c, 2 * c), 0)
    si = lax.broadcasted_iota(jnp.int32, (c, 2 * c), 1) % c
    level_masks = [None if 2 * half == c else
                   (ti == si if half == 0 else (ti // (2 * half)) == (si // (2 * half)))
                   for half in HG_LEVELS]
    sub3 = lax.broadcasted_iota(jnp.int32, (1, 8, HG_DK), 1)
    chunks = [slice(c * ch, c * (ch + 1)) for ch in range(tm // c)]

    og = og_ref[...]
    y_parts = []

    def scan(pairs):
        heads = [hd for hp in pairs for hd in (2 * hp, 2 * hp + 1)]
        flat = [(hd, r) for hd in heads for r in chunks]
        outs, states = _hgrn_heads([q_s[hd, r, :] for hd, r in flat],
                                   [lf_s[hd, r, :] for hd, r in flat],
                                   [k_s[hd, r, :] for hd, r in flat],
                                   [v_s[hd, r, :] for hd, r in flat],
                                   [st_ref[hd] for hd in heads], tri3, level_masks, sub3)
        for hd, st in zip(heads, states):
            st_ref[hd] = st
        normed = []
        for n, hd in enumerate(heads):
            oh = jnp.concatenate(outs[len(chunks) * n:len(chunks) * (n + 1)], axis=0)
            ms = jnp.mean(oh * oh, axis=-1, keepdims=True)
            normed.append(oh * lax.rsqrt(ms + EPS) * og[:, HG_DK * hd:HG_DK * (hd + 1)])
        for n, hp in enumerate(pairs):
            cols = slice(2 * HG_DK * hp, 2 * HG_DK * (hp + 1))
            on = jnp.concatenate(normed[2 * n:2 * n + 2], axis=1) * z_s[:, cols]
            y_parts.append(jnp.dot(on.astype(BF16), wout_ref[cols, :], preferred_element_type=F32))

    groups = [list(range(g, g + HG_GROUP // 2)) for g in range(0, nh // 2, HG_GROUP // 2)]
    for n, pairs in enumerate(groups):
        for hp in pairs:
            for sect in range(4):
                project(sect, hp)
        if n > 0:
            scan(groups[n - 1])
    scan(groups[-1])
    o_ref[0] = x + gate * functools.reduce(lambda a, b: a + b, y_parts)


def _const_spec(shape):
    return pl.BlockSpec(shape, lambda b, i: (0,) * len(shape), pipeline_mode=pl.Buffered(1))


def _hgrn_layer(x, mod, norm_g, w_in, lb, o_g, w_out, lb_is_zero):
    bsz, seq, d = x.shape
    tm = HG_SEQ_TILE
    nh = d // HG_DK
    slab = pltpu.VMEM((nh, tm, HG_DK), F32)
    return pl.pallas_call(
        functools.partial(_hgrn_kernel, lb_is_zero),
        grid=(bsz, seq // tm),
        in_specs=[
            pl.BlockSpec((1, tm, d), lambda b, i: (b, i, 0)),
            pl.BlockSpec((1, 3, d), lambda b, i: (b, 0, 0)),
            _const_spec((1, d)),
            _const_spec((d, 4 * d)),
            _const_spec((1, d)),
            _const_spec((1, d)),
            _const_spec((d, d)),
        ],
        out_specs=pl.BlockSpec((1, tm, d), lambda b, i: (b, i, 0)),
        out_shape=jax.ShapeDtypeStruct(x.shape, F32),
        scratch_shapes=[
            pltpu.VMEM((nh, HG_DK, HG_DK), F32),
            slab, slab, slab, slab,
            pltpu.VMEM((tm, d), F32),
        ],
        compiler_params=pltpu.CompilerParams(
            dimension_semantics=("arbitrary", "arbitrary"), vmem_limit_bytes=VMEM_LIMIT_BYTES),
        name="hgrn_layer",
    )(x, mod, norm_g.reshape(1, d), w_in.astype(BF16), lb.reshape(1, d), o_g.reshape(1, d),
      w_out.astype(BF16))


def _fox_kernel(fixed_ref, x_ref, mod_ref, ng_ref, wqvzt_ref, wk_ref, wf_ref,
                bf_ref, qg_ref, kg_ref, seg_ref, segt_ref, tri3_ref, negsel_ref, bound_ref, wout_ref,
                o_ref, k_s, cp_s, vt_s, carry_s, sz_s, wq_s, m_s, l_s, acc_s):
    b = pl.program_id(0)
    i = pl.program_id(1)
    tm = x_ref.shape[1]
    d = x_ref.shape[2]
    nh = d // FX_HD
    npair = nh // 2
    tk = tm
    row0 = pl.multiple_of(i * tm, tm)

    @pl.when((b == 0) & (i == 0))
    def _():
        wq_s[:, :LANES, :] = jnp.zeros((nh, LANES, tm), BF16)
        wq_s[:, LANES:, :] = negsel_ref[...]

    x = x_ref[0]
    shift, scale, gate = mod_ref[0, 0:1, :], mod_ref[0, 1:2, :], mod_ref[0, 2:3, :]
    h = _modulated_norm(x, ng_ref[...], scale, shift).astype(BF16)

    lane = lax.broadcasted_iota(jnp.int32, (tm, LANES), 1)
    fl = jnp.dot(h, wf_ref[...], preferred_element_type=F32)
    k = jnp.dot(h, wk_ref[...], preferred_element_type=F32)
    ss = jnp.dot((k * k).astype(BF16), seg_ref[...], preferred_element_type=F32)
    qvzt = lax.dot_general(wqvzt_ref[...], h, _NT, preferred_element_type=F32)
    lf = jnp.where(lane < nh, _log_sigmoid(fl + bf_ref[...]), 0.0)
    hi, mid, lo = _split3(lf)
    cum = jnp.dot(tri3_ref[...], jnp.concatenate([hi, mid, lo], axis=0),
                  preferred_element_type=F32)
    r = lax.rsqrt(ss * (1.0 / FX_HD) + EPS)
    r_hi = r.astype(BF16)
    r_lo = (r - r_hi.astype(F32)).astype(BF16)
    rr = (jnp.dot(r_hi, segt_ref[...], preferred_element_type=F32)
          + jnp.dot(r_lo, segt_ref[...], preferred_element_type=F32))

    qt3 = qvzt[:d].reshape(nh, FX_HD, tm)
    vt_s[:, pl.ds(row0, tm)] = qvzt[d:2 * d].astype(BF16)
    sz_s[...] = _silu(qvzt[2 * d:])

    qms = jnp.mean(qt3 * qt3, axis=1, keepdims=True)
    qgain = (qg_ref[...] * (FX_HD ** -0.5 * LOG2E)).reshape(nh, FX_HD, 1)
    qtn = (qt3 * lax.rsqrt(qms + EPS) * qgain).astype(BF16)
    for hd in range(nh):
        wq_s[hd, FX_HD * (hd % 2):FX_HD * (hd % 2 + 1), :] = qtn[hd]

    kn = (k * rr * kg_ref[...]).astype(BF16)
    for p in range(npair):
        k_s[p, pl.ds(row0, tm), :] = kn[:, LANES * p:LANES * (p + 1)]

    cum = cum + jnp.where(i == 0, 0.0, carry_s[0:1, :])
    carry_s[...] = jnp.broadcast_to(cum[tm - 1:tm, :], carry_s.shape)
    chi, cmid, clo = _split3(cum * LOG2E)
    cp = (chi.astype(F32) + pltpu.roll(cmid.astype(F32), nh, axis=1)
          + pltpu.roll(clo.astype(F32), 2 * nh, axis=1))
    if fixed_ref:
        cp = cp + jnp.where((lane >= 3 * nh) & (lane < 3 * nh + 3), 1.0, 0.0)
        ref = jnp.transpose(cum * LOG2E)[:nh, :] - bound_ref[...]
        sub16 = lax.broadcasted_iota(jnp.int32, (16, tm), 0)
        ref_pieces = [p.astype(F32) for p in _split3(ref)]
        for hd in range(nh):
            rhi, rmid, rlo = [p[hd:hd + 1, :] for p in ref_pieces]
            blk = jnp.where(sub16 == 0, rhi, jnp.where(sub16 == 1, rmid,
                                                       jnp.where(sub16 == 2, rlo, 0.0)))
            wq_s[hd, LANES + 3 * nh:LANES + 3 * nh + 16, :] = blk.astype(BF16)
    else:
        m_s[...] = jnp.full(m_s.shape, -jnp.inf, F32)
    cp_s[pl.ds(row0, tm), :] = cp.astype(BF16)

    l_s[...] = jnp.zeros_like(l_s)
    acc_s[...] = jnp.zeros_like(acc_s)
    key_i = lax.broadcasted_iota(jnp.int32, (tk, tm), 0)
    qry_i = lax.broadcasted_iota(jnp.int32, (tk, tm), 1)
    causal = key_i <= qry_i

    def score_fn(j):
        rows = pl.ds(pl.multiple_of(j * tk, tk), tk)
        lhs = {}

        def score(hd):
            p = hd // 2
            if p not in lhs:
                lhs[p] = jnp.concatenate([k_s[p, rows, :], cp_s[rows, :]], axis=1)
            return jnp.dot(lhs[p], wq_s[hd], preferred_element_type=F32)
        return score

    def tile_update(j, ahead, last, head_done=None):
        rows = pl.ds(pl.multiple_of(j * tk, tk), tk)
        score, score_next = score_fn(j), score_fn(j + 1)
        scores = dict(enumerate(ahead))
        ahead_next = []
        for hd in range(nh):
            if head_done is not None and hd > 0:
                head_done(hd - 1)
            nxt = hd + FX_LOOKAHEAD
            if nxt < nh:
                scores[nxt] = score(nxt)
            elif not last:
                ahead_next.append(score_next(nxt - nh))
            st = scores.pop(hd)
            if last:
                st = jnp.where(causal, st, -jnp.inf)
            st = st.reshape(tk // 8, 8, tm)
            hs = slice(FX_HD * hd, FX_HD * (hd + 1))
            if fixed_ref:
                pr = jnp.exp2(st)
                l_s[hd] = l_s[hd] + jnp.sum(pr, axis=0)
                acc_s[hs, :] = acc_s[hs, :] + jnp.dot(
                    vt_s[hs, rows], pr.reshape(tk, tm).astype(BF16), preferred_element_type=F32)
                continue
            m_old = m_s[hd]
            mx = jnp.max(st, axis=0)
            mx = jnp.broadcast_to(jnp.max(mx, axis=0, keepdims=True), (8, tm))
            m_new = jnp.maximum(m_old, mx)
            alpha = jnp.exp2(m_old - m_new)
            pr = jnp.exp2(st - m_new)
            m_s[hd] = m_new
            l_s[hd] = alpha * l_s[hd] + jnp.sum(pr, axis=0)
            pv = jnp.dot(vt_s[hs, rows], pr.reshape(tk, tm).astype(BF16),
                         preferred_element_type=F32)
            acc_s[hs, :] = jnp.tile(alpha, (FX_HD // 8, 1)) * acc_s[hs, :] + pv
        if head_done is not None:
            head_done(nh - 1)
        return tuple(ahead_next)

    first = score_fn(0)
    ahead = lax.fori_loop(
        0, i // 2,
        lambda jj, ahead: tile_update(2 * jj + 1, tile_update(2 * jj, ahead, False), False),
        tuple(first(hd) for hd in range(FX_LOOKAHEAD)))
    ahead = lax.cond(i % 2 == 1, lambda ahead: tile_update(i - 1, ahead, False),
                     lambda ahead: ahead, ahead)
    group = 2 * LANES // FX_HD
    y_parts = []

    def project_group(g):
        rows_g = slice(2 * LANES * g, 2 * LANES * (g + 1))
        ot = jnp.concatenate(
            [acc_s[FX_HD * hd:FX_HD * (hd + 1), :] / jnp.sum(l_s[hd], axis=0, keepdims=True)
             for hd in range(group * g, group * (g + 1))], axis=0)
        on = (ot * sz_s[rows_g, :]).astype(BF16)
        y_parts.append(lax.dot_general(on, wout_ref[rows_g, :], _TN,
                                       preferred_element_type=F32))

    def head_done(hd):
        if (hd + 1) % group == 0 and hd + 1 >= 2 * group:
            project_group((hd + 1) // group - 2)

    tile_update(i, ahead, True, head_done)
    project_group(nh // group - 1)
    o_ref[0] = x + gate * functools.reduce(lambda a, b: a + b, y_parts)


def _fox_layer(x, mod, norm_g, w_in, b_f, q_g, k_g, w_out):
    bsz, seq, d = x.shape
    tm = FX_SEQ_TILE
    nh = d // FX_HD
    npair = nh // 2
    wq, wk, wv, wz, wf = (w_in[:, :d], w_in[:, d:2 * d], w_in[:, 2 * d:3 * d],
                          w_in[:, 3 * d:4 * d], w_in[:, 4 * d:])
    head_of = jnp.arange(d) // FX_HD
    seg = (head_of[:, None] == jnp.arange(LANES)[None, :]).astype(BF16)
    tri = (jnp.arange(tm)[None, :] <= jnp.arange(tm)[:, None]).astype(BF16)
    tri3 = jnp.concatenate([tri, tri, tri], axis=1)
    sel_row = jnp.arange(LANES)[None, :, None]
    sel_head = jnp.arange(nh)[:, None, None]
    negsel = -((sel_row % nh == sel_head) & (sel_row < 3 * nh)).astype(BF16)
    negsel = jnp.broadcast_to(negsel, (nh, LANES, tm))
    wf_pad = jnp.pad(wf, ((0, 0), (0, LANES - nh))).astype(BF16)
    bf_pad = jnp.pad(b_f, (0, LANES - nh)).reshape(1, LANES)
    bound = 1.01 * FX_HD ** 0.5 * LOG2E * jnp.max(jnp.abs(q_g)) * jnp.max(jnp.abs(k_g))
    operands = (x, mod, norm_g.reshape(1, d), jnp.concatenate([wq, wv, wz], axis=1).T.astype(BF16),
                wk.astype(BF16), wf_pad, bf_pad, jnp.tile(q_g, nh).reshape(d, 1),
                jnp.tile(k_g, nh).reshape(1, d), seg, seg.T, tri3, negsel, bound.reshape(1, 1),
                w_out.astype(BF16))
    return lax.cond(2.0 * bound <= FX_FIXED_REF_SPAN,
                    functools.partial(_fox_call, True), functools.partial(_fox_call, False),
                    *operands)


def _fox_call(fixed_ref, *operands):
    bsz, seq, d = operands[0].shape
    tm = FX_SEQ_TILE
    nh = d // FX_HD
    npair = nh // 2
    return pl.pallas_call(
        functools.partial(_fox_kernel, fixed_ref),
        grid=(bsz, seq // tm),
        in_specs=[
            pl.BlockSpec((1, tm, d), lambda b, i: (b, i, 0)),
            pl.BlockSpec((1, 3, d), lambda b, i: (b, 0, 0)),
            _const_spec((1, d)),
            _const_spec((3 * d, d)),
            _const_spec((d, d)),
            _const_spec((d, LANES)),
            _const_spec((1, LANES)),
            _const_spec((d, 1)),
            _const_spec((1, d)),
            _const_spec((d, LANES)),
            _const_spec((LANES, d)),
            _const_spec((tm, 3 * tm)),
            _const_spec((nh, LANES, tm)),
            _const_spec((1, 1)),
            _const_spec((d, d)),
        ],
        out_specs=pl.BlockSpec((1, tm, d), lambda b, i: (b, i, 0)),
        out_shape=jax.ShapeDtypeStruct(operands[0].shape, F32),
        scratch_shapes=[
            pltpu.VMEM((npair, seq, LANES), BF16),
            pltpu.VMEM((seq, LANES), BF16),
            pltpu.VMEM((d, seq), BF16),
            pltpu.VMEM((8, LANES), F32),
            pltpu.VMEM((d, tm), F32),
            pltpu.VMEM((nh, 2 * LANES, tm), BF16),
            pltpu.VMEM((nh, 8, tm), F32),
            pltpu.VMEM((nh, 8, tm), F32),
            pltpu.VMEM((d, tm), F32),
        ],
        compiler_params=pltpu.CompilerParams(
            dimension_semantics=("arbitrary", "arbitrary"), vmem_limit_bytes=VMEM_LIMIT_BYTES),
        name="fox_layer_fixed_ref" if fixed_ref else "fox_layer_running_max",
    )(*operands)


def kernel(x, c, norm_g, ada_w, ada_b, hg_lb_logits, hg_w_in, hg_o_g, hg_w_out,
           fx_w_in, fx_b_f, fx_q_g, fx_k_g, fx_w_out):
    depth = norm_g.shape[0]
    bsz, _, d = x.shape
    lb_all = jnp.cumsum(jax.nn.softmax(hg_lb_logits.astype(F32), axis=0), axis=0)
    lb_all = lb_all - lb_all[0:1]
    mod = _adaln(c, ada_w, ada_b).reshape(depth, bsz, 3, d)
    for layer in range(depth):
        j = layer // 2
        if layer % 2 == 0:
            x = _hgrn_layer(x, mod[layer], norm_g[layer], hg_w_in[j], lb_all[j], hg_o_g[j],
                            hg_w_out[j], lb_is_zero=(j == 0))
        else:
            x = _fox_layer(x, mod[layer], norm_g[layer], fx_w_in[j], fx_b_f[j], fx_q_g[j],
                           fx_k_g[j], fx_w_out[j])
    return x
```

```python
import functools

import jax
import jax.numpy as jnp
from jax import lax
from jax.experimental import pallas as pl
from jax.experimental.pallas import tpu as pltpu

F32 = jnp.float32
BF16 = jnp.bfloat16

EPS = 1e-6
LOG2E = 1.4426950408889634
LANES = 128
HG_DK = 128
HG_CHUNK = 64
HG_LEVELS = (32, 16, 8, 4, 2, 1, 0)
HG_GROUP = 8
FX_HD = 64
FX_LOOKAHEAD = 4
FX_FIXED_REF_SPAN = 96.0
HG_SEQ_TILE = 512
FX_SEQ_TILE = 256
VMEM_LIMIT_BYTES = 56 * 1024 * 1024

_NT = (((1,), (1,)), ((), ()))
_TN = (((0,), (0,)), ((), ()))


def _split3(x):
    hi = x.astype(BF16)
    r1 = x - hi.astype(F32)
    mid = r1.astype(BF16)
    lo = (r1 - mid.astype(F32)).astype(BF16)
    return hi, mid, lo


def _modulated_norm(x, g, scale, shift):
    ms = jnp.mean(x * x, axis=-1, keepdims=True)
    return (x * lax.rsqrt(ms + EPS) * g) * (1.0 + scale) + shift


def _silu(x):
    return x / (1.0 + jnp.exp(-x))


def _log_sigmoid(x):
    return jnp.minimum(x, 0.0) - jnp.log(1.0 + jnp.exp(-jnp.abs(x)))


def _adaln_kernel(c_ref, w_ref, b_ref, o_ref):
    c_act = _silu(c_ref[...])
    o_ref[0] = jnp.dot(c_act, w_ref[0], precision=lax.Precision.HIGHEST,
                       preferred_element_type=F32) + b_ref[0]


def _adaln(c, ada_w, ada_b):
    depth, d, n = ada_w.shape
    b = c.shape[0]
    tn = 1024
    return pl.pallas_call(
        _adaln_kernel,
        grid=(depth, n // tn),
        in_specs=[
            pl.BlockSpec((b, d), lambda l, j: (0, 0)),
            pl.BlockSpec((1, d, tn), lambda l, j: (l, 0, j)),
            pl.BlockSpec((1, 1, tn), lambda l, j: (l, 0, j)),
        ],
        out_specs=pl.BlockSpec((1, b, tn), lambda l, j: (l, 0, j)),
        out_shape=jax.ShapeDtypeStruct((depth, b, n), F32),
        compiler_params=pltpu.CompilerParams(
            dimension_semantics=("arbitrary", "arbitrary"), vmem_limit_bytes=VMEM_LIMIT_BYTES),
        name="adaln_mod",
    )(c, ada_w, ada_b.reshape(depth, 1, n))


def _hgrn_level_operands(q, k, b, lf, half, sub3):
    c = HG_CHUNK
    if half == 0:
        return q, k
    if half >= 8:
        qp, kp = [], []
        for g in range(c // (2 * half)):
            lo_r = slice(2 * half * g, 2 * half * g + half)
            up_r = slice(2 * half * g + half, 2 * half * (g + 1))
            ref = b[2 * half * g + half - 1:2 * half * g + half, :]
            zeros = jnp.zeros((half, HG_DK), F32)
            qp += [zeros, q[up_r] * jnp.exp2(b[up_r] - ref)]
            kp += [k[lo_r] * jnp.exp2(ref - b[lo_r]), zeros]
        return jnp.concatenate(qp, axis=0), jnp.concatenate(kp, axis=0)
    shape3 = (c // 8, 8, HG_DK)
    q3, k3, b3 = q.reshape(shape3), k.reshape(shape3), b.reshape(shape3)
    if half == 1:
        upper = sub3 % 2 == 1
        qp = jnp.where(upper, q3 * jnp.exp2(lf.reshape(shape3)), 0.0)
        kp = jnp.where(upper, 0.0, k3)
    else:
        if half == 4:
            ref = b3[:, 3:4, :]
        else:
            ref = jnp.where(sub3 < 4, b3[:, 1:2, :], b3[:, 5:6, :])
        upper = sub3 % (2 * half) >= half
        w = jnp.exp2(-jnp.abs(b3 - ref))
        qp = jnp.where(upper, q3 * w, 0.0)
        kp = jnp.where(upper, 0.0, k3 * w)
    return qp.reshape(c, HG_DK), kp.reshape(c, HG_DK)


def _blockdiag(a, b):
    z = jnp.zeros_like(a)
    return jnp.concatenate([jnp.concatenate([a, z], axis=1), jnp.concatenate([z, b], axis=1)],
                           axis=0)


def _hgrn_heads(qs, lfs, ks, vs, states, tri3, level_masks, sub3):
    c = HG_CHUNK
    n = len(qs)
    per_head = n // len(states)
    pairs = [(i, i + 1) for i in range(0, n, 2)]
    lo, hi = slice(0, HG_DK), slice(HG_DK, 2 * HG_DK)
    bs = [None] * n
    for a, b in pairs:
        rhs = jnp.concatenate([jnp.concatenate(_split3(lfs[a]), axis=0),
                               jnp.concatenate(_split3(lfs[b]), axis=0)], axis=1)
        b2 = jnp.dot(tri3, rhs, preferred_element_type=F32)
        bs[a], bs[b] = b2[:, lo], b2[:, hi]
    vbs = [v.astype(BF16) for v in vs]
    o_intra = [None] * n
    a_sums = []
    for a, b in pairs:
        a_sum = None
        for half, mask in zip(HG_LEVELS, level_masks):
            qa, ka = _hgrn_level_operands(qs[a], ks[a], bs[a], lfs[a], half, sub3)
            qb, kb = _hgrn_level_operands(qs[b], ks[b], bs[b], lfs[b], half, sub3)
            a_l = lax.dot_general(jnp.concatenate([qa, qb], axis=1).astype(BF16),
                                  _blockdiag(ka.astype(BF16), kb.astype(BF16)), _NT,
                                  preferred_element_type=F32)
            a_l = a_l if mask is None else jnp.where(mask, a_l, 0.0)
            a_sum = a_l if a_sum is None else a_sum + a_l
        a_sums.append(a_sum)
    for (a, b), a_sum in zip(pairs, a_sums):
        o2 = jnp.dot(a_sum.astype(BF16), _blockdiag(vbs[a], vbs[b]), preferred_element_type=F32)
        o_intra[a], o_intra[b] = o2[:, lo], o2[:, hi]
    blasts = [b[c - 1:c, :] for b in bs]
    q1s = [(qs[i] * jnp.exp2(bs[i])).astype(BF16) for i in range(n)]
    kds = [(ks[i] * jnp.exp2(blasts[i] - bs[i])).astype(BF16) for i in range(n)]
    pcs = [None] * n
    for a, b in pairs:
        p2 = lax.dot_general(jnp.concatenate([vbs[a], vbs[b]], axis=0), _blockdiag(kds[a], kds[b]),
                             _TN, preferred_element_type=F32)
        pcs[a], pcs[b] = p2[:, lo], p2[:, hi]
    st_in, st_out = [], []
    for h, st in enumerate(states):
        for i in range(per_head * h, per_head * (h + 1)):
            st_in.append(st)
            st = st * jnp.exp2(blasts[i]) + pcs[i]
        st_out.append(st)
    outs = [None] * n
    for a, b in pairs:
        i2 = lax.dot_general(jnp.concatenate([q1s[a], q1s[b]], axis=1),
                             _blockdiag(st_in[a].astype(BF16), st_in[b].astype(BF16)), _NT,
                             preferred_element_type=F32)
        outs[a], outs[b] = o_intra[a] + i2[:, lo], o_intra[b] + i2[:, hi]
    return outs, st_out


def _hgrn_kernel(lb_is_zero, x_ref, mod_ref, ng_ref, win_ref, lb_ref, og_ref, wout_ref, o_ref,
                 st_ref, q_s, lf_s, k_s, v_s, z_s, oh_s):
    i = pl.program_id(1)
    tm = x_ref.shape[1]
    d = x_ref.shape[2]
    nh = d // HG_DK
    c = HG_CHUNK

    @pl.when(i == 0)
    def _():
        st_ref[...] = jnp.zeros_like(st_ref)

    x = x_ref[0]
    shift, scale, gate = mod_ref[0, 0:1, :], mod_ref[0, 1:2, :], mod_ref[0, 2:3, :]
    h = _modulated_norm(x, ng_ref[...], scale, shift).astype(BF16)

    if not lb_is_zero:
        lb = lb_ref[...]
        l0, l1, oml = jnp.log(lb), jnp.log1p(-lb), 1.0 - lb
    blocks = [(sect, hp) for hp in range(nh // 2) for sect in range(3)]
    blocks += [(3, hp) for hp in range(nh // 2)]
    for sect, hp in blocks:
        blk = sect * (nh // 2) + hp
        p = jnp.dot(h, win_ref[:, 256 * blk:256 * (blk + 1)], preferred_element_type=F32)
        for hh in range(2):
            head = 2 * hp + hh
            ph = p[:, HG_DK * hh:HG_DK * (hh + 1)]
            if sect == 0:
                q_s[head] = ph
            elif sect == 1:
                lbh = slice(HG_DK * head, HG_DK * (head + 1))
                e = jnp.exp(-jnp.abs(ph))
                sig_neg = jnp.where(ph >= 0.0, e, 1.0) / (1.0 + e)
                if lb_is_zero:
                    lf_s[head] = jnp.minimum(ph, 0.0) * LOG2E - jnp.log2(1.0 + e)
                    k_s[head] = sig_neg
                else:
                    cc = l1[:, lbh] + (jnp.minimum(ph, 0.0) - jnp.log(1.0 + e))
                    hi_arg = jnp.maximum(l0[:, lbh], cc)
                    lf_s[head] = (hi_arg + jnp.log(1.0 + jnp.exp(-jnp.abs(l0[:, lbh] - cc)))) * LOG2E
                    k_s[head] = oml[:, lbh] * sig_neg
            elif sect == 2:
                v_s[head] = ph
            else:
                z_s[:, HG_DK * head:HG_DK * (head + 1)] = _silu(ph)

    ri = lax.broadcasted_iota(jnp.int32, (c, 3 * c), 0)
    ci = lax.broadcasted_iota(jnp.int32, (c, 3 * c), 1)
    tri3 = ((ci % c) <= ri).astype(BF16)
    ti = lax.broadcasted_iota(jnp.int32, (c, 2 * c), 0)
    si = lax.broadcasted_iota(jnp.int32, (c, 2 * c), 1) % c
    level_masks = [None if 2 * half == c else
                   (ti == si if half == 0 else (ti // (2 * half)) == (si // (2 * half)))
                   for half in HG_LEVELS]
    sub3 = lax.broadcasted_iota(jnp.int32, (1, 8, HG_DK), 1)
    chunks = [slice(c * ch, c * (ch + 1)) for ch in range(tm // c)]

    def group_body(g, carry):
        heads = [g * HG_GROUP + hh for hh in range(HG_GROUP)]
        flat = [(hd, r) for hd in heads for r in chunks]
        outs, states = _hgrn_heads([q_s[hd, r, :] for hd, r in flat],
                                   [lf_s[hd, r, :] for hd, r in flat],
                                   [k_s[hd, r, :] for hd, r in flat],
                                   [v_s[hd, r, :] for hd, r in flat],
                                   [st_ref[hd] for hd in heads], tri3, level_masks, sub3)
        for (hd, r), o_c in zip(flat, outs):
            oh_s[hd, r, :] = o_c
        for hd, st in zip(heads, states):
            st_ref[hd] = st
        return carry

    lax.fori_loop(0, nh // HG_GROUP, group_body, 0)

    og = og_ref[...]
    y = None
    for hp in range(nh // 2):
        outs = []
        for head in (2 * hp, 2 * hp + 1):
            oh = oh_s[head]
            ms = jnp.mean(oh * oh, axis=-1, keepdims=True)
            outs.append(oh * lax.rsqrt(ms + EPS) * og[:, HG_DK * head:HG_DK * (head + 1)])
        cols = slice(2 * HG_DK * hp, 2 * HG_DK * (hp + 1))
        on = jnp.concatenate(outs, axis=1) * z_s[:, cols]
        y_p = jnp.dot(on.astype(BF16), wout_ref[cols, :], preferred_element_type=F32)
        y = y_p if y is None else y + y_p
    o_ref[0] = x + gate * y


def _const_spec(shape):
    return pl.BlockSpec(shape, lambda b, i: (0,) * len(shape), pipeline_mode=pl.Buffered(1))


def _hgrn_layer(x, mod, norm_g, w_in, lb, o_g, w_out, lb_is_zero):
    bsz, seq, d = x.shape
    tm = HG_SEQ_TILE
    nh = d // HG_DK
    slab = pltpu.VMEM((nh, tm, HG_DK), F32)
    return pl.pallas_call(
        functools.partial(_hgrn_kernel, lb_is_zero),
        grid=(bsz, seq // tm),
        in_specs=[
            pl.BlockSpec((1, tm, d), lambda b, i: (b, i, 0)),
            pl.BlockSpec((1, 3, d), lambda b, i: (b, 0, 0)),
            _const_spec((1, d)),
            _const_spec((d, 4 * d)),
            _const_spec((1, d)),
            _const_spec((1, d)),
            _const_spec((d, d)),
        ],
        out_specs=pl.BlockSpec((1, tm, d), lambda b, i: (b, i, 0)),
        out_shape=jax.ShapeDtypeStruct(x.shape, F32),
        scratch_shapes=[
            pltpu.VMEM((nh, HG_DK, HG_DK), F32),
            slab, slab, slab, slab,
            pltpu.VMEM((tm, d), F32),
            slab,
        ],
        compiler_params=pltpu.CompilerParams(
            dimension_semantics=("arbitrary", "arbitrary"), vmem_limit_bytes=VMEM_LIMIT_BYTES),
        name="hgrn_layer",
    )(x, mod, norm_g.reshape(1, d), w_in.astype(BF16), lb.reshape(1, d), o_g.reshape(1, d),
      w_out.astype(BF16))


def _fox_kernel(fixed_ref, x_ref, mod_ref, ng_ref, wqvzt_ref, wk_ref, wf_ref,
                bf_ref, qg_ref, kg_ref, seg_ref, segt_ref, tri3_ref, negsel_ref, bound_ref, wout_ref,
                o_ref, k_s, cp_s, vt_s, carry_s, sz_s, wq_s, m_s, l_s, acc_s):
    b = pl.program_id(0)
    i = pl.program_id(1)
    tm = x_ref.shape[1]
    d = x_ref.shape[2]
    nh = d // FX_HD
    npair = nh // 2
    tk = tm
    row0 = pl.multiple_of(i * tm, tm)

    @pl.when((b == 0) & (i == 0))
    def _():
        wq_s[:, :LANES, :] = jnp.zeros((nh, LANES, tm), BF16)
        wq_s[:, LANES:, :] = negsel_ref[...]

    x = x_ref[0]
    shift, scale, gate = mod_ref[0, 0:1, :], mod_ref[0, 1:2, :], mod_ref[0, 2:3, :]
    h = _modulated_norm(x, ng_ref[...], scale, shift).astype(BF16)

    lane = lax.broadcasted_iota(jnp.int32, (tm, LANES), 1)
    fl = jnp.dot(h, wf_ref[...], preferred_element_type=F32)
    k = jnp.dot(h, wk_ref[...], preferred_element_type=F32)
    ss = jnp.dot((k * k).astype(BF16), seg_ref[...], preferred_element_type=F32)
    qvzt = lax.dot_general(wqvzt_ref[...], h, _NT, preferred_element_type=F32)
    lf = jnp.where(lane < nh, _log_sigmoid(fl + bf_ref[...]), 0.0)
    hi, mid, lo = _split3(lf)
    cum = jnp.dot(tri3_ref[...], jnp.concatenate([hi, mid, lo], axis=0),
                  preferred_element_type=F32)
    r = lax.rsqrt(ss * (1.0 / FX_HD) + EPS)
    r_hi = r.astype(BF16)
    r_lo = (r - r_hi.astype(F32)).astype(BF16)
    rr = (jnp.dot(r_hi, segt_ref[...], preferred_element_type=F32)
          + jnp.dot(r_lo, segt_ref[...], preferred_element_type=F32))

    qt3 = qvzt[:d].reshape(nh, FX_HD, tm)
    vt_s[:, pl.ds(row0, tm)] = qvzt[d:2 * d].astype(BF16)
    sz_s[...] = _silu(qvzt[2 * d:])

    qms = jnp.mean(qt3 * qt3, axis=1, keepdims=True)
    qgain = (qg_ref[...] * (FX_HD ** -0.5 * LOG2E)).reshape(nh, FX_HD, 1)
    qtn = (qt3 * lax.rsqrt(qms + EPS) * qgain).astype(BF16)
    for hd in range(nh):
        wq_s[hd, FX_HD * (hd % 2):FX_HD * (hd % 2 + 1), :] = qtn[hd]

    kn = (k * rr * kg_ref[...]).astype(BF16)
    for p in range(npair):
        k_s[p, pl.ds(row0, tm), :] = kn[:, LANES * p:LANES * (p + 1)]

    cum = cum + jnp.where(i == 0, 0.0, carry_s[0:1, :])
    carry_s[...] = jnp.broadcast_to(cum[tm - 1:tm, :], carry_s.shape)
    chi, cmid, clo = _split3(cum * LOG2E)
    cp = (chi.astype(F32) + pltpu.roll(cmid.astype(F32), nh, axis=1)
          + pltpu.roll(clo.astype(F32), 2 * nh, axis=1))
    if fixed_ref:
        cp = cp + jnp.where((lane >= 3 * nh) & (lane < 3 * nh + 3), 1.0, 0.0)
        ref = jnp.transpose(cum * LOG2E)[:nh, :] - bound_ref[...]
        sub16 = lax.broadcasted_iota(jnp.int32, (16, tm), 0)
        ref_pieces = [p.astype(F32) for p in _split3(ref)]
        for hd in range(nh):
            rhi, rmid, rlo = [p[hd:hd + 1, :] for p in ref_pieces]
            blk = jnp.where(sub16 == 0, rhi, jnp.where(sub16 == 1, rmid,
                                                       jnp.where(sub16 == 2, rlo, 0.0)))
            wq_s[hd, LANES + 3 * nh:LANES + 3 * nh + 16, :] = blk.astype(BF16)
    else:
        m_s[...] = jnp.full(m_s.shape, -jnp.inf, F32)
    cp_s[pl.ds(row0, tm), :] = cp.astype(BF16)

    l_s[...] = jnp.zeros_like(l_s)
    acc_s[...] = jnp.zeros_like(acc_s)
    key_i = lax.broadcasted_iota(jnp.int32, (tk, tm), 0)
    qry_i = lax.broadcasted_iota(jnp.int32, (tk, tm), 1)
    causal = key_i <= qry_i

    def score_fn(j):
        rows = pl.ds(pl.multiple_of(j * tk, tk), tk)
        lhs = {}

        def score(hd):
            p = hd // 2
            if p not in lhs:
                lhs[p] = jnp.concatenate([k_s[p, rows, :], cp_s[rows, :]], axis=1)
            return jnp.dot(lhs[p], wq_s[hd], preferred_element_type=F32)
        return score

    def tile_update(j, ahead, last, head_done=None):
        rows = pl.ds(pl.multiple_of(j * tk, tk), tk)
        score, score_next = score_fn(j), score_fn(j + 1)
        scores = dict(enumerate(ahead))
        ahead_next = []
        for hd in range(nh):
            if head_done is not None and hd > 0:
                head_done(hd - 1)
            nxt = hd + FX_LOOKAHEAD
            if nxt < nh:
                scores[nxt] = score(nxt)
            elif not last:
                ahead_next.append(score_next(nxt - nh))
            st = scores.pop(hd)
            if last:
                st = jnp.where(causal, st, -jnp.inf)
            st = st.reshape(tk // 8, 8, tm)
            hs = slice(FX_HD * hd, FX_HD * (hd + 1))
            if fixed_ref:
                pr = jnp.exp2(st)
                l_s[hd] = l_s[hd] + jnp.sum(pr, axis=0)
                acc_s[hs, :] = acc_s[hs, :] + jnp.dot(
                    vt_s[hs, rows], pr.reshape(tk, tm).astype(BF16), preferred_element_type=F32)
                continue
            m_old = m_s[hd]
            mx = jnp.max(st, axis=0)
            mx = jnp.broadcast_to(jnp.max(mx, axis=0, keepdims=True), (8, tm))
            m_new = jnp.maximum(m_old, mx)
            alpha = jnp.exp2(m_old - m_new)
            pr = jnp.exp2(st - m_new)
            m_s[hd] = m_new
            l_s[hd] = alpha * l_s[hd] + jnp.sum(pr, axis=0)
            pv = jnp.dot(vt_s[hs, rows], pr.reshape(tk, tm).astype(BF16),
                         preferred_element_type=F32)
            acc_s[hs, :] = jnp.tile(alpha, (FX_HD // 8, 1)) * acc_s[hs, :] + pv
        if head_done is not None:
            head_done(nh - 1)
        return tuple(ahead_next)

    first = score_fn(0)
    ahead = lax.fori_loop(
        0, i // 2,
        lambda jj, ahead: tile_update(2 * jj + 1, tile_update(2 * jj, ahead, False), False),
        tuple(first(hd) for hd in range(FX_LOOKAHEAD)))
    ahead = lax.cond(i % 2 == 1, lambda ahead: tile_update(i - 1, ahead, False),
                     lambda ahead: ahead, ahead)
    group = 2 * LANES // FX_HD
    y_parts = []

    def project_group(g):
        rows_g = slice(2 * LANES * g, 2 * LANES * (g + 1))
        ot = jnp.concatenate(
            [acc_s[FX_HD * hd:FX_HD * (hd + 1), :] / jnp.sum(l_s[hd], axis=0, keepdims=True)
             for hd in range(group * g, group * (g + 1))], axis=0)
        on = (ot * sz_s[rows_g, :]).astype(BF16)
        y_parts.append(lax.dot_general(on, wout_ref[rows_g, :], _TN,
                                       preferred_element_type=F32))

    def head_done(hd):
        if (hd + 1) % group == 0 and hd + 1 >= 2 * group:
            project_group((hd + 1) // group - 2)

    tile_update(i, ahead, True, head_done)
    project_group(nh // group - 1)
    o_ref[0] = x + gate * functools.reduce(lambda a, b: a + b, y_parts)


def _fox_layer(x, mod, norm_g, w_in, b_f, q_g, k_g, w_out):
    bsz, seq, d = x.shape
    tm = FX_SEQ_TILE
    nh = d // FX_HD
    npair = nh // 2
    wq, wk, wv, wz, wf = (w_in[:, :d], w_in[:, d:2 * d], w_in[:, 2 * d:3 * d],
                          w_in[:, 3 * d:4 * d], w_in[:, 4 * d:])
    head_of = jnp.arange(d) // FX_HD
    seg = (head_of[:, None] == jnp.arange(LANES)[None, :]).astype(BF16)
    tri = (jnp.arange(tm)[None, :] <= jnp.arange(tm)[:, None]).astype(BF16)
    tri3 = jnp.concatenate([tri, tri, tri], axis=1)
    sel_row = jnp.arange(LANES)[None, :, None]
    sel_head = jnp.arange(nh)[:, None, None]
    negsel = -((sel_row % nh == sel_head) & (sel_row < 3 * nh)).astype(BF16)
    negsel = jnp.broadcast_to(negsel, (nh, LANES, tm))
    wf_pad = jnp.pad(wf, ((0, 0), (0, LANES - nh))).astype(BF16)
    bf_pad = jnp.pad(b_f, (0, LANES - nh)).reshape(1, LANES)
    bound = 1.01 * FX_HD ** 0.5 * LOG2E * jnp.max(jnp.abs(q_g)) * jnp.max(jnp.abs(k_g))
    operands = (x, mod, norm_g.reshape(1, d), jnp.concatenate([wq, wv, wz], axis=1).T.astype(BF16),
                wk.astype(BF16), wf_pad, bf_pad, jnp.tile(q_g, nh).reshape(d, 1),
                jnp.tile(k_g, nh).reshape(1, d), seg, seg.T, tri3, negsel, bound.reshape(1, 1),
                w_out.astype(BF16))
    return lax.cond(2.0 * bound <= FX_FIXED_REF_SPAN,
                    functools.partial(_fox_call, True), functools.partial(_fox_call, False),
                    *operands)


def _fox_call(fixed_ref, *operands):
    bsz, seq, d = operands[0].shape
    tm = FX_SEQ_TILE
    nh = d // FX_HD
    npair = nh // 2
    return pl.pallas_call(
        functools.partial(_fox_kernel, fixed_ref),
        grid=(bsz, seq // tm),
        in_specs=[
            pl.BlockSpec((1, tm, d), lambda b, i: (b, i, 0)),
            pl.BlockSpec((1, 3, d), lambda b, i: (b, 0, 0)),
            _const_spec((1, d)),
            _const_spec((3 * d, d)),
            _const_spec((d, d)),
            _const_spec((d, LANES)),
            _const_spec((1, LANES)),
            _const_spec((d, 1)),
            _const_spec((1, d)),
            _const_spec((d, LANES)),
            _const_spec((LANES, d)),
            _const_spec((tm, 3 * tm)),
            _const_spec((nh, LANES, tm)),
            _const_spec((1, 1)),
            _const_spec((d, d)),
        ],
        out_specs=pl.BlockSpec((1, tm, d), lambda b, i: (b, i, 0)),
        out_shape=jax.ShapeDtypeStruct(operands[0].shape, F32),
        scratch_shapes=[
            pltpu.VMEM((npair, seq, LANES), BF16),
            pltpu.VMEM((seq, LANES), BF16),
            pltpu.VMEM((d, seq), BF16),
            pltpu.VMEM((8, LANES), F32),
            pltpu.VMEM((d, tm), F32),
            pltpu.VMEM((nh, 2 * LANES, tm), BF16),
            pltpu.VMEM((nh, 8, tm), F32),
            pltpu.VMEM((nh, 8, tm), F32),
            pltpu.VMEM((d, tm), F32),
        ],
        compiler_params=pltpu.CompilerParams(
            dimension_semantics=("arbitrary", "arbitrary"), vmem_limit_bytes=VMEM_LIMIT_BYTES),
        name="fox_layer_fixed_ref" if fixed_ref else "fox_layer_running_max",
    )(*operands)


def kernel(x, c, norm_g, ada_w, ada_b, hg_lb_logits, hg_w_in, hg_o_g, hg_w_out,
           fx_w_in, fx_b_f, fx_q_g, fx_k_g, fx_w_out):
    depth = norm_g.shape[0]
    bsz, _, d = x.shape
    lb_all = jnp.cumsum(jax.nn.softmax(hg_lb_logits.astype(F32), axis=0), axis=0)
    lb_all = lb_all - lb_all[0:1]
    mod = _adaln(c, ada_w, ada_b).reshape(depth, bsz, 3, d)
    for layer in range(depth):
        j = layer // 2
        if layer % 2 == 0:
            x = _hgrn_layer(x, mod[layer], norm_g[layer], hg_w_in[j], lb_all[j], hg_o_g[j],
                            hg_w_out[j], lb_is_zero=(j == 0))
        else:
            x = _fox_layer(x, mod[layer], norm_g[layer], fx_w_in[j], fx_b_f[j], fx_q_g[j],
                           fx_k_g[j], fx_w_out[j])
    return x
```
